```python
import math
import jax, jax.numpy as jnp
from jax import lax
import numpy as np

D_MODEL = 1024
BATCH = 8
SEQ = 8192
DEPTH = 2

ATT_HEADS = D_MODEL // 128
ATT_HEAD_DIM = 64
ATT_V_DIM = 2 * ATT_HEAD_DIM
ATT_WIDTH = ATT_HEADS * ATT_V_DIM
Q_BLOCK = 128
ALIBI_MAX_BIAS = 8.0

GDN_HEADS = D_MODEL // 128
GDN_DK = 128
GDN_DV = 128
GDN_WIDTH = GDN_HEADS * GDN_DV
CONV_WIDTH = 4
CHUNK = 64

N_GROUPS = 4
EXPERTS_PER_GROUP = 8
N_EXPERTS = N_GROUPS * EXPERTS_PER_GROUP
TOP_K = 2
D_EXPERT = D_MODEL // 2
MOE_BLOCK = 128

NORM_EPS = 1e-6
NEG_INF = -1e30
ADA_INIT = 0.01

P_QA = ATT_HEADS * 2 * ATT_HEAD_DIM
P_KA = ATT_HEADS * 2 * ATT_HEAD_DIM
P_VA = ATT_WIDTH
P_QKV_B = GDN_HEADS * (2 * GDN_DK + GDN_DV)
P_ZB = GDN_WIDTH
P_BETA = GDN_HEADS
P_DECAY = GDN_HEADS
P_GATES = 2 * D_MODEL
SPLIT_SIZES = (P_QA, P_KA, P_VA, P_QKV_B, P_ZB, P_BETA, P_DECAY, P_GATES)
SPLIT_POINTS = tuple(sum(SPLIT_SIZES[:i + 1]) for i in range(len(SPLIT_SIZES) - 1))
P_TOTAL = sum(SPLIT_SIZES)

kernel_name = 'hybrid_diffattn_gdn_hier_moe'


def rms_norm(x, w):
    xf = x.astype(jnp.float32)
    y = xf * lax.rsqrt(jnp.mean(xf * xf, axis=-1, keepdims=True) + NORM_EPS)
    return (y * w.astype(jnp.float32)).astype(x.dtype)


def l2_norm(x):
    return x * lax.rsqrt(jnp.sum(x * x, axis=-1, keepdims=True) + NORM_EPS)


def modulate(h, shift, scale):
    return h * (1 + scale[:, None, :]) + shift[:, None, :]


def causal_depthwise_conv(x, w):
    k_len, ch = w.shape
    return lax.conv_general_dilated(x, w[:, None, :].astype(x.dtype), window_strides=(1,),
                                    padding=((k_len - 1, 0),), dimension_numbers=('NWC', 'WIO', 'NWC'),
                                    feature_group_count=ch)


def diff_attention(q1, q2, k1, k2, v, lam):
    b, h, s, d = q1.shape
    scale = d ** -0.5
    slopes = jnp.exp2(-ALIBI_MAX_BIAS * jnp.arange(1, h + 1, dtype=jnp.float32) / h)
    kpos = jnp.arange(s)

    def block(i):
        start = i * Q_BLOCK
        qb1 = lax.dynamic_slice_in_dim(q1, start, Q_BLOCK, axis=2)
        qb2 = lax.dynamic_slice_in_dim(q2, start, Q_BLOCK, axis=2)
        dist = (start + jnp.arange(Q_BLOCK))[:, None] - kpos[None, :]
        bias = jnp.where(dist >= 0, -slopes[:, None, None] * dist.astype(jnp.float32), NEG_INF)
        s1 = jnp.einsum('bhqd,bhkd->bhqk', qb1, k1).astype(jnp.float32) * scale + bias
        s2 = jnp.einsum('bhqd,bhkd->bhqk', qb2, k2).astype(jnp.float32) * scale + bias
        p = jax.nn.softmax(s1, axis=-1) - lam * jax.nn.softmax(s2, axis=-1)
        return jnp.einsum('bhqk,bhke->bhqe', p.astype(v.dtype), v)

    out = lax.map(block, jnp.arange(s // Q_BLOCK))
    return jnp.moveaxis(out, 0, 2).reshape(b, h, s, v.shape[-1])


def diff_attention_branch(qa, ka, va, lq1, lk1, lq2, lk2, subln_w, lam_init):
    b, s, _ = qa.shape
    q = qa.reshape(b, s, ATT_HEADS, 2, ATT_HEAD_DIM).transpose(0, 2, 3, 1, 4)
    k = ka.reshape(b, s, ATT_HEADS, 2, ATT_HEAD_DIM).transpose(0, 2, 3, 1, 4)
    v = va.reshape(b, s, ATT_HEADS, ATT_V_DIM).transpose(0, 2, 1, 3)
    lam = (jnp.exp(jnp.dot(lq1.astype(jnp.float32), lk1.astype(jnp.float32)))
           - jnp.exp(jnp.dot(lq2.astype(jnp.float32), lk2.astype(jnp.float32))) + lam_init)
    o = diff_attention(q[:, :, 0], q[:, :, 1], k[:, :, 0], k[:, :, 1], v, lam)
    o = rms_norm(o, subln_w) * (1 - lam_init)
    return o.transpose(0, 2, 1, 3).reshape(b, s, ATT_WIDTH)


def gated_delta_rule(q, k, v, g, beta):
    b, h, s, dk = q.shape
    dv = v.shape[-1]
    n = s // CHUNK
    q = q * dk ** -0.5

    def chunks(t):
        return t.reshape(b, h, n, CHUNK, *t.shape[3:])

    q, k, v, beta = chunks(q), chunks(k), chunks(v), chunks(beta)
    g = jnp.cumsum(chunks(g), axis=-1)
    kb = k * beta[..., None]
    vb = v * beta[..., None]
    incl = jnp.tril(jnp.ones((CHUNK, CHUNK), dtype=bool))
    strict = jnp.tril(jnp.ones((CHUNK, CHUNK), dtype=bool), -1)
    diff = g[..., :, None] - g[..., None, :]
    decay = jnp.where(incl, jnp.exp(jnp.where(incl, diff, 0.0)), 0.0)
    a_mat = jnp.eye(CHUNK, dtype=jnp.float32) + jnp.where(
        strict, jnp.einsum('bhncd,bhnsd->bhncs', kb, k) * decay, 0.0)
    u = lax.linalg.triangular_solve(a_mat, vb, left_side=True, lower=True, unit_diagonal=True)
    w = lax.linalg.triangular_solve(a_mat, kb * jnp.exp(g)[..., None], left_side=True, lower=True,
                                    unit_diagonal=True)
    xs = tuple(jnp.moveaxis(t, 2, 0) for t in (q, k, u, w, g, decay))

    def step(state, inp):
        qc, kc, uc, wc, gc, dc = inp
        v_new = uc - jnp.einsum('bhck,bhkv->bhcv', wc, state)
        intra = jnp.einsum('bhck,bhsk->bhcs', qc, kc) * dc
        o = (jnp.einsum('bhck,bhkv->bhcv', qc * jnp.exp(gc)[..., None], state)
             + jnp.einsum('bhcs,bhsv->bhcv', intra, v_new))
        g_last = gc[..., -1]
        state = (state * jnp.exp(g_last)[..., None, None]
                 + jnp.einsum('bhck,bhcv->bhkv', kc * jnp.exp(g_last[..., None] - gc)[..., None], v_new))
        return state, o

    state0 = jnp.zeros((b, h, dk, dv), jnp.float32)
    _, o = lax.scan(step, state0, xs)
    return jnp.moveaxis(o, 0, 2).reshape(b, h, s, dv)


def gated_deltanet_branch(qkv_raw, z, b_raw, a_raw, conv_w, a_log, dt_bias, norm_w):
    bsz, s, _ = qkv_raw.shape
    qkv = jax.nn.silu(causal_depthwise_conv(qkv_raw, conv_w))
    q, k, v = jnp.split(qkv, [GDN_HEADS * GDN_DK, 2 * GDN_HEADS * GDN_DK], axis=-1)

    def heads(t, dh):
        return t.reshape(bsz, s, GDN_HEADS, dh).transpose(0, 2, 1, 3).astype(jnp.float32)

    q = l2_norm(heads(q, GDN_DK))
    k = l2_norm(heads(k, GDN_DK))
    v = heads(v, GDN_DV)
    beta = jax.nn.sigmoid(b_raw.astype(jnp.float32)).transpose(0, 2, 1)
    g = (-jnp.exp(a_log.astype(jnp.float32))
         * jax.nn.softplus(a_raw.astype(jnp.float32) + dt_bias.astype(jnp.float32))).transpose(0, 2, 1)
    o = gated_delta_rule(q, k, v, g, beta).transpose(0, 2, 1, 3)
    zf = z.reshape(bsz, s, GDN_HEADS, GDN_DV).astype(jnp.float32)
    o = rms_norm(o, norm_w) * jax.nn.silu(zf)
    return o.reshape(bsz, s, GDN_WIDTH).astype(qkv_raw.dtype)


def dispatch_experts(xt, expert_ids, weights, w1, w3, w2):
    n_tok, d = xt.shape
    flat_e = expert_ids.reshape(-1).astype(jnp.int32)
    n_assign = flat_e.shape[0]
    order = jnp.argsort(flat_e)
    sorted_e = flat_e[order]
    counts = jnp.bincount(flat_e, length=N_EXPERTS)
    padded = (counts + MOE_BLOCK - 1) // MOE_BLOCK * MOE_BLOCK
    pad_end = jnp.cumsum(padded)
    pad_start = pad_end - padded
    start = jnp.cumsum(counts) - counts
    dest = pad_start[sorted_e] + jnp.arange(n_assign, dtype=jnp.int32) - start[sorted_e]
    n_blocks = -(-n_assign // MOE_BLOCK) + N_EXPERTS
    n_rows = n_blocks * MOE_BLOCK
    row_tok = jnp.full((n_rows,), n_tok, jnp.int32).at[dest].set((order // TOP_K).astype(jnp.int32))
    row_w = jnp.zeros((n_rows,), xt.dtype).at[dest].set(weights.reshape(-1)[order].astype(xt.dtype))
    block_e = jnp.minimum(jnp.searchsorted(pad_end, jnp.arange(n_blocks, dtype=jnp.int32) * MOE_BLOCK,
                                           side='right'), N_EXPERTS - 1)
    x_rows = jnp.concatenate([xt, jnp.zeros((1, d), xt.dtype)], axis=0)[row_tok]
    x_rows = x_rows.reshape(n_blocks, MOE_BLOCK, d)

    def expert_block(args):
        xb, e = args
        return (jax.nn.silu(xb @ w1[e]) * (xb @ w3[e])) @ w2[e]

    y_rows = lax.map(expert_block, (x_rows, block_e)).reshape(n_rows, d) * row_w[:, None]
    return jnp.zeros((n_tok + 1, d), xt.dtype).at[row_tok].add(y_rows)[:n_tok]


def hierarchical_moe(h, w_rg, b_rg, w_re, b_re, w1, w3, w2):
    bsz, s, d = h.shape
    xt = h.reshape(-1, d)
    n_tok = xt.shape[0]
    gprob = jax.nn.softmax((xt @ w_rg).astype(jnp.float32) + b_rg.astype(jnp.float32), axis=-1)
    gsel = jnp.argmax(gprob, axis=-1)
    gweight = jnp.take_along_axis(gprob, gsel[:, None], axis=-1)
    elog = ((xt @ w_re).astype(jnp.float32) + b_re.astype(jnp.float32)).reshape(
        n_tok, N_GROUPS, EXPERTS_PER_GROUP)
    elog = jnp.take_along_axis(elog, gsel[:, None, None], axis=1)[:, 0]
    top_p, top_i = lax.top_k(jax.nn.softmax(elog, axis=-1), TOP_K)
    weights = gweight * top_p / jnp.sum(top_p, axis=-1, keepdims=True)
    expert_ids = gsel[:, None] * EXPERTS_PER_GROUP + top_i
    return dispatch_experts(xt, expert_ids, weights, w1, w3, w2).reshape(bsz, s, d)


def setup_inputs(seed: int = 0) -> dict:
    key = jax.random.key(seed)
    ks = jax.random.split(key, 32)
    f32 = jnp.float32
    d, l = D_MODEL, DEPTH

    def nrm(k, shape, s):
        return jax.random.normal(k, shape, f32) * s

    return {
        'x': nrm(ks[0], (BATCH, SEQ, d), 1.0),
        'c': nrm(ks[1], (BATCH, d), 1.0),
        'ada_w': nrm(ks[2], (l, d, 6 * d), ADA_INIT),
        'ada_b': nrm(ks[3], (l, 6 * d), 0.01),
        'norm1_w': 1.0 + nrm(ks[4], (l, d), 0.01),
        'w_in': nrm(ks[5], (l, d, P_TOTAL), d ** -0.5),
        'conv_w': nrm(ks[6], (l, CONV_WIDTH, P_QKV_B), CONV_WIDTH ** -0.5),
        'lambda_q1': nrm(ks[7], (l, ATT_HEAD_DIM), 0.1),
        'lambda_k1': nrm(ks[8], (l, ATT_HEAD_DIM), 0.1),
        'lambda_q2': nrm(ks[9], (l, ATT_HEAD_DIM), 0.1),
        'lambda_k2': nrm(ks[10], (l, ATT_HEAD_DIM), 0.1),
        'subln_w': 1.0 + nrm(ks[11], (l, ATT_V_DIM), 0.01),
        'a_log': jnp.log(jax.random.uniform(ks[12], (l, GDN_HEADS), f32, 1.0, 16.0)),
        'dt_bias': nrm(ks[13], (l, GDN_HEADS), 0.1),
        'gdn_norm_w': 1.0 + nrm(ks[14], (l, GDN_DV), 0.01),
        'w_branch_a': nrm(ks[15], (l, ATT_WIDTH, d), ATT_WIDTH ** -0.5),
        'w_branch_b': nrm(ks[16], (l, GDN_WIDTH, d), GDN_WIDTH ** -0.5),
        'w_out': nrm(ks[17], (l, d, d), d ** -0.5),
        'norm2_w': 1.0 + nrm(ks[18], (l, d), 0.01),
        'router_group_w': nrm(ks[19], (l, d, N_GROUPS), d ** -0.5),
        'router_group_b': nrm(ks[20], (l, N_GROUPS), 0.01),
        'router_expert_w': nrm(ks[21], (l, d, N_EXPERTS), d ** -0.5),
        'router_expert_b': nrm(ks[22], (l, N_EXPERTS), 0.01),
        'expert_w1': nrm(ks[23], (l, N_EXPERTS, d, D_EXPERT), d ** -0.5),
        'expert_w3': nrm(ks[24], (l, N_EXPERTS, d, D_EXPERT), d ** -0.5),
        'expert_w2': nrm(ks[25], (l, N_EXPERTS, D_EXPERT, d), D_EXPERT ** -0.5),
        'final_norm_w': 1.0 + nrm(ks[26], (d,), 0.01),
    }


def reference(x, c, ada_w, ada_b, norm1_w, w_in, conv_w, lambda_q1, lambda_k1, lambda_q2, lambda_k2,
              subln_w, a_log, dt_bias, gdn_norm_w, w_branch_a, w_branch_b, w_out, norm2_w,
              router_group_w, router_group_b, router_expert_w, router_expert_b,
              expert_w1, expert_w3, expert_w2, final_norm_w):
    cond = jax.nn.silu(c)
    for l in range(DEPTH):
        lam_init = 0.8 - 0.6 * math.exp(-0.3 * l)
        mod = cond @ ada_w[l] + ada_b[l]
        shift1, scale1, gate1, shift2, scale2, gate2 = jnp.split(mod, 6, axis=-1)

        h = modulate(rms_norm(x, norm1_w[l]), shift1, scale1)
        proj = h @ w_in[l]
        qa, ka, va, qkv_b, zb, beta_b, decay_b, gates = jnp.split(proj, SPLIT_POINTS, axis=-1)
        y_a = diff_attention_branch(qa, ka, va, lambda_q1[l], lambda_k1[l], lambda_q2[l], lambda_k2[l],
                                    subln_w[l], lam_init)
        y_b = gated_deltanet_branch(qkv_b, zb, beta_b, decay_b, conv_w[l], a_log[l], dt_bias[l],
                                    gdn_norm_w[l])
        gate_a, gate_b = jnp.split(jax.nn.sigmoid(gates), 2, axis=-1)
        mixed = gate_a * (y_a @ w_branch_a[l]) + gate_b * (y_b @ w_branch_b[l])
        x = x + gate1[:, None, :] * (mixed @ w_out[l])

        h2 = modulate(rms_norm(x, norm2_w[l]), shift2, scale2)
        y_ffn = hierarchical_moe(h2, router_group_w[l], router_group_b[l], router_expert_w[l],
                                 router_expert_b[l], expert_w1[l], expert_w3[l], expert_w2[l])
        x = x + gate2[:, None, :] * y_ffn
    return rms_norm(x, final_norm_w)
```

```python
import functools
import math

import jax
import jax.numpy as jnp
from jax import lax
from jax.experimental import pallas as pl
from jax.experimental.pallas import tpu as pltpu

F32 = jnp.float32
BF16 = jnp.bfloat16
HIGHEST = lax.Precision.HIGHEST

D_MODEL = 1024
LANES = 128
ATT_HEADS = 8
ATT_HEAD_DIM = 64
ALIBI_MAX_BIAS = 8.0
GDN_HEADS = 8
GDN_DK = 128
CONV_WIDTH = 4
GDN_CHUNK = 128
N_GROUPS = 4
EXPERTS_PER_GROUP = 8
N_EXPERTS = N_GROUPS * EXPERTS_PER_GROUP
TOP_K = 2
D_EXPERT = D_MODEL // 2
NORM_EPS = 1e-6
NEG_INF = -1e30

C_QA, C_KA, C_VA = 0, 1024, 2048
C_QB, C_KB, C_VB = 3072, 4096, 5120
C_ZB = 6144
C_GATES = 7168
P_MAIN = 9216

ATT_TQ = 256
ATT_TK = 512
MOE_ROWS = 256
TOK_TILE = 256
VMEM_LIMIT = 56 * 1024 * 1024


def _cparams(sem):
    return pltpu.CompilerParams(dimension_semantics=sem, vmem_limit_bytes=VMEM_LIMIT)


def _sigmoid(x):
    return 1.0 / (1.0 + jnp.exp(-x))


def _silu(x):
    return x * _sigmoid(x)


def _dot(a, b):
    return jnp.dot(a, b, preferred_element_type=F32)


def _dot_nt(a, b):
    return lax.dot_general(a, b, (((1,), (1,)), ((), ())), preferred_element_type=F32)


def _ada_kernel(c_ref, w_ref, b_ref, o_ref):
    cond = _silu(c_ref[...])
    o_ref[0] = jnp.dot(cond, w_ref[0], preferred_element_type=F32, precision=HIGHEST) + b_ref[0]


def _ada_mod(c, ada_w, ada_b):
    depth, d, d6 = ada_w.shape
    bsz = c.shape[0]
    tn = 1536
    return pl.pallas_call(
        _ada_kernel,
        out_shape=jax.ShapeDtypeStruct((depth, bsz, d6), F32),
        grid=(depth, d6 // tn),
        in_specs=[pl.BlockSpec((bsz, d), lambda l, j: (0, 0)),
                  pl.BlockSpec((1, d, tn), lambda l, j: (l, 0, j)),
                  pl.BlockSpec((1, 1, tn), lambda l, j: (l, 0, j))],
        out_specs=pl.BlockSpec((1, bsz, tn), lambda l, j: (l, 0, j)),
        compiler_params=_cparams(("arbitrary", "arbitrary")),
        name="ada_mod",
    )(c, ada_w, ada_b.reshape(depth, 1, d6))


def _inproj_kernel(x_ref, shift_ref, scale_ref, nw_ref, w_ref, ws_ref, o_ref, os_ref, hn_ref):
    @pl.when(pl.program_id(1) == 0)
    def _():
        x = x_ref[...]
        ms = jnp.mean(x * x, axis=-1, keepdims=True)
        hn = x * lax.rsqrt(ms + NORM_EPS) * nw_ref[...]
        hn = hn * (1.0 + scale_ref[0]) + shift_ref[0]
        hn_ref[...] = hn.astype(BF16)
        os_ref[...] = jnp.dot(hn, ws_ref[...], preferred_element_type=F32, precision=HIGHEST)

    o_ref[...] = _dot(hn_ref[...], w_ref[...]).astype(BF16)


def _in_proj(x2, shift, scale, norm_w, w_main, w_small, seq):
    n, d = x2.shape
    tm = min(1024, seq)
    tn = 1024
    per_b = seq // tm
    return pl.pallas_call(
        _inproj_kernel,
        out_shape=(jax.ShapeDtypeStruct((n, P_MAIN), BF16), jax.ShapeDtypeStruct((n, LANES), F32)),
        grid=(n // tm, P_MAIN // tn),
        in_specs=[pl.BlockSpec((tm, d), lambda i, j: (i, 0)),
                  pl.BlockSpec((1, 1, d), lambda i, j: (i // per_b, 0, 0)),
                  pl.BlockSpec((1, 1, d), lambda i, j: (i // per_b, 0, 0)),
                  pl.BlockSpec((1, d), lambda i, j: (0, 0)),
                  pl.BlockSpec((d, tn), lambda i, j: (0, j)),
                  pl.BlockSpec((d, LANES), lambda i, j: (0, 0))],
        out_specs=(pl.BlockSpec((tm, tn), lambda i, j: (i, j)),
                   pl.BlockSpec((tm, LANES), lambda i, j: (i, 0))),
        scratch_shapes=[pltpu.VMEM((tm, d), BF16)],
        compiler_params=_cparams(("parallel", "arbitrary")),
        name="in_proj",
    )(x2, shift, scale, norm_w, w_main, w_small)


def _attn_kernel(slope_ref, q_ref, k_ref, v_ref, lamv_ref, subw_ref, o_ref,
                 qs_ref, bias_ref, m_ref, l_ref, acc_ref, *, tq, tk, lam_init):
    h = pl.program_id(1)
    qi = pl.program_id(2)
    slope = slope_ref[h]
    rows = 2 * tq

    @pl.when(qi == 0)
    def _():
        r = lax.broadcasted_iota(jnp.int32, (rows, tk), 0)
        c = lax.broadcasted_iota(jnp.int32, (rows, tk), 1)
        rel = c - jnp.where(r >= tq, r - tq, r)
        bias_ref[...] = rel.astype(F32) * slope

    q = q_ref[0].astype(F32) * (ATT_HEAD_DIM ** -0.5)
    lane = lax.broadcasted_iota(jnp.int32, (tq, LANES), 1)
    qs_ref[0:tq, :] = jnp.where(lane < ATT_HEAD_DIM, q, 0.0).astype(BF16)
    qs_ref[tq:rows, :] = jnp.where(lane >= ATT_HEAD_DIM, q, 0.0).astype(BF16)
    m_ref[...] = jnp.full((rows, LANES), NEG_INF, F32)
    l_ref[...] = jnp.zeros((rows, LANES), F32)
    acc_ref[...] = jnp.zeros((rows, LANES), F32)

    q_start = qi * tq
    n_full = q_start // tk

    def block(j, masked):
        k_start = pl.multiple_of(j * tk, tk)
        kb = k_ref[0, pl.ds(k_start, tk), :]
        vb = v_ref[0, pl.ds(k_start, tk), :]
        s = _dot_nt(qs_ref[...], kb) + bias_ref[...]
        if masked:
            r = lax.broadcasted_iota(jnp.int32, (rows, tk), 0)
            c = lax.broadcasted_iota(jnp.int32, (rows, tk), 1)
            rel = c - jnp.where(r >= tq, r - tq, r)
            s = jnp.where(rel <= q_start - k_start, s, NEG_INF)
        delta = slope * (k_start - q_start).astype(F32)
        m_old = m_ref[...]
        m_new = jnp.maximum(m_old, jnp.max(s, axis=1, keepdims=True) + delta)
        alpha = jnp.exp(m_old - m_new)
        p = jnp.exp(s + (delta - m_new[:, 0:1]))
        l_ref[...] = alpha * l_ref[...] + jnp.sum(p, axis=1, keepdims=True)
        acc_ref[...] = alpha * acc_ref[...] + _dot(p.astype(BF16), vb)
        m_ref[...] = m_new

    def loop_body(j, carry):
        block(j, False)
        return carry

    lax.fori_loop(0, n_full, loop_body, 0)
    block(n_full, True)

    lamv = lamv_ref[...]
    lam = (jnp.exp(jnp.sum(lamv[0:1] * lamv[1:2], axis=1, keepdims=True))
           - jnp.exp(jnp.sum(lamv[2:3] * lamv[3:4], axis=1, keepdims=True)) + lam_init)
    acc = acc_ref[...]
    l = l_ref[...]
    o = acc[0:tq] / l[0:tq] - lam * (acc[tq:rows] / l[tq:rows])
    y = o * lax.rsqrt(jnp.mean(o * o, axis=-1, keepdims=True) + NORM_EPS)
    o_ref[0] = (y * subw_ref[...] * (1.0 - lam_init)).astype(BF16)


def _diff_attention(proj3, lamv, subln_w, lam_init):
    bsz, seq, _ = proj3.shape
    tq = min(ATT_TQ, seq)
    tk = min(ATT_TK, seq)
    slopes = jnp.exp2(-ALIBI_MAX_BIAS * jnp.arange(1, ATT_HEADS + 1, dtype=F32) / ATT_HEADS)
    kern = functools.partial(_attn_kernel, tq=tq, tk=tk, lam_init=lam_init)
    nq = C_QA // LANES
    nk = C_KA // LANES
    nv = C_VA // LANES
    return pl.pallas_call(
        kern,
        out_shape=jax.ShapeDtypeStruct((bsz, seq, ATT_HEADS * LANES), BF16),
        grid_spec=pltpu.PrefetchScalarGridSpec(
            num_scalar_prefetch=1,
            grid=(bsz, ATT_HEADS, seq // tq),
            in_specs=[pl.BlockSpec((1, tq, LANES), lambda b, h, i, s: (b, i, nq + h)),
                      pl.BlockSpec((1, seq, LANES), lambda b, h, i, s: (b, 0, nk + h)),
                      pl.BlockSpec((1, seq, LANES), lambda b, h, i, s: (b, 0, nv + h)),
                      pl.BlockSpec((4, LANES), lambda b, h, i, s: (0, 0)),
                      pl.BlockSpec((1, LANES), lambda b, h, i, s: (0, 0))],
            out_specs=pl.BlockSpec((1, tq, LANES), lambda b, h, i, s: (b, i, h)),
            scratch_shapes=[pltpu.VMEM((2 * tq, LANES), BF16),
                            pltpu.VMEM((2 * tq, tk), F32),
                            pltpu.VMEM((2 * tq, LANES), F32),
                            pltpu.VMEM((2 * tq, LANES), F32),
                            pltpu.VMEM((2 * tq, LANES), F32)]),
        compiler_params=_cparams(("parallel", "arbitrary", "arbitrary")),
        name="diff_attn",
    )(slopes, proj3, proj3, proj3, lamv, subln_w)


def _gdn_kernel(q_ref, k_ref, v_ref, z_ref, small_ref, cwq_ref, cwk_ref, cwv_ref,
                alog_ref, dtb_ref, nw_ref, o_ref, xbuf_ref, state_ref, *, heads, chunk):
    c_idx = pl.program_id(1)
    width = heads * LANES
    pad = 8

    @pl.when(c_idx == 0)
    def _():
        xbuf_ref[:, 0:pad, :] = jnp.zeros((3, pad, width), F32)
        state_ref[...] = jnp.zeros(state_ref.shape, F32)

    def conv_silu(idx, raw_ref, cw_ref):
        xbuf_ref[idx, pad:pad + chunk, :] = raw_ref[0].astype(F32)
        cw = cw_ref[...]
        acc = xbuf_ref[idx, pad:pad + chunk, :] * cw[CONV_WIDTH - 1:CONV_WIDTH]
        for back in range(1, CONV_WIDTH):
            tap = CONV_WIDTH - 1 - back
            acc = acc + xbuf_ref[idx, pad - back:pad - back + chunk, :] * cw[tap:tap + 1]
        xbuf_ref[idx, 0:pad, :] = xbuf_ref[idx, chunk:chunk + pad, :]
        return _silu(acc)

    qc = conv_silu(0, q_ref, cwq_ref)
    kc = conv_silu(1, k_ref, cwk_ref)
    vc = conv_silu(2, v_ref, cwv_ref)

    small = small_ref[0]
    beta_all = _sigmoid(small)
    sp_in = small + dtb_ref[...]
    softplus = jnp.maximum(sp_in, 0.0) + jnp.log(1.0 + jnp.exp(-jnp.abs(sp_in)))
    g_all = -jnp.exp(alog_ref[...]) * softplus
    row = lax.broadcasted_iota(jnp.int32, (chunk, chunk), 0)
    col = lax.broadcasted_iota(jnp.int32, (chunk, chunk), 1)
    incl = col <= row
    strict = col < row
    tri = jnp.where(incl, 1.0, 0.0).astype(F32)
    gc_all = jnp.dot(tri, g_all, preferred_element_type=F32, precision=HIGHEST)
    eye = jnp.where(row == col, 1.0, 0.0).astype(F32)

    level_masks = []
    for k in range(chunk.bit_length() - 1):
        same = (row >> (k + 1)) == (col >> (k + 1))
        level_masks.append(same & (((row >> k) & 1) == 1) & (((col >> k) & 1) == 0))

    nw = nw_ref[...]
    for hh in range(heads):
        sl = slice(hh * LANES, (hh + 1) * LANES)
        qh, kh, vh = qc[:, sl], kc[:, sl], vc[:, sl]
        qn = qh * lax.rsqrt(jnp.sum(qh * qh, axis=-1, keepdims=True) + NORM_EPS) * (GDN_DK ** -0.5)
        kn = kh * lax.rsqrt(jnp.sum(kh * kh, axis=-1, keepdims=True) + NORM_EPS)
        beta = beta_all[:, hh:hh + 1]
        gc = gc_all[:, heads + hh:heads + hh + 1]
        g_col = jnp.broadcast_to(gc, (chunk, chunk))
        g_row = g_col.T
        decay = jnp.where(incl, jnp.exp(jnp.where(incl, g_col - g_row, 0.0)), 0.0)
        kb = kn * beta
        kn16 = kn.astype(BF16)
        kk = _dot_nt(kb.astype(BF16), kn16)
        qk = _dot_nt(qn.astype(BF16), kn16)
        low = jnp.where(strict, kk * decay, 0.0)
        intra = qk * decay
        inv = eye - jnp.where(level_masks[0], low, 0.0)
        for m in level_masks[1:]:
            e = jnp.where(m, low, 0.0).astype(BF16)
            inv16 = inv.astype(BF16)
            inv = inv - _dot(inv16, _dot(e, inv16).astype(BF16))
        eg = jnp.exp(gc)
        rhs = jnp.concatenate([vh * beta, kb * eg], axis=1).astype(BF16)
        uw = _dot(inv.astype(BF16), rhs)
        u, w = uw[:, 0:LANES], uw[:, LANES:2 * LANES]
        state = state_ref[hh]
        state16 = state.astype(BF16)
        v_new = u - _dot(w.astype(BF16), state16)
        v_new16 = v_new.astype(BF16)
        o = _dot((qn * eg).astype(BF16), state16) + _dot(intra.astype(BF16), v_new16)
        g_last = gc[chunk - 1:chunk, :]
        kdec = kn * jnp.exp(g_last - gc)
        state_ref[hh] = state * jnp.exp(g_last) + _dot(kdec.T.astype(BF16), v_new16)
        y = o * lax.rsqrt(jnp.mean(o * o, axis=-1, keepdims=True) + NORM_EPS) * nw
        zf = z_ref[0, :, sl].astype(F32)
        o_ref[0, :, sl] = (y * _silu(zf)).astype(BF16)


def _gated_deltanet(proj3, small3, conv_w, a_log, dt_bias, norm_w):
    bsz, seq, _ = proj3.shape
    heads = GDN_HEADS
    width = heads * LANES
    chunk = min(GDN_CHUNK, seq)
    zeros = jnp.zeros((LANES - 2 * heads,), F32)
    alog_row = jnp.concatenate([jnp.zeros((heads,), F32), a_log, zeros]).reshape(1, LANES)
    dtb_row = jnp.concatenate([jnp.zeros((heads,), F32), dt_bias, zeros]).reshape(1, LANES)
    kern = functools.partial(_gdn_kernel, heads=heads, chunk=chunk)
    blk = lambda off: pl.BlockSpec((1, chunk, width), lambda b, c: (b, c, off // width))
    cw = lambda j: pl.BlockSpec((CONV_WIDTH, width), lambda b, c: (0, j))
    row = pl.BlockSpec((1, LANES), lambda b, c: (0, 0))
    return pl.pallas_call(
        kern,
        out_shape=jax.ShapeDtypeStruct((bsz, seq, width), BF16),
        grid=(bsz, seq // chunk),
        in_specs=[blk(C_QB), blk(C_KB), blk(C_VB), blk(C_ZB),
                  pl.BlockSpec((1, chunk, LANES), lambda b, c: (b, c, 0)),
                  cw(0), cw(1), cw(2), row, row, row],
        out_specs=pl.BlockSpec((1, chunk, width), lambda b, c: (b, c, 0)),
        scratch_shapes=[pltpu.VMEM((3, chunk + 8, width), F32),
                        pltpu.VMEM((heads, GDN_DK, LANES), F32)],
        compiler_params=_cparams(("parallel", "arbitrary")),
        name="gated_deltanet",
    )(proj3, proj3, proj3, proj3, small3, conv_w, conv_w, conv_w, alog_row, dtb_row, norm_w)


def _pack_bf16_pairs(x):
    k = x.shape[1] // 2
    r = pltpu.bitcast(x.astype(BF16).astype(F32), jnp.uint32)
    return (r[:, k:] & jnp.uint32(0xFFFF0000)) | (r[:, :k] >> 16)


def _unpack_bf16_pairs(p):
    lo = pltpu.bitcast(p << 16, F32)
    hi = pltpu.bitcast(p & jnp.uint32(0xFFFF0000), F32)
    return jnp.concatenate([lo, hi], axis=1).astype(BF16)


def _merge_kernel(ya_ref, yb_ref, ga_ref, gb_ref, x_ref, gate1_ref, nw_ref, shift_ref, scale_ref,
                  wa_ref, wb_ref, wo_ref, wr_ref, xo_ref, h2_ref, lg_ref):
    a = _dot(ya_ref[...], wa_ref[...])
    b = _dot(yb_ref[...], wb_ref[...])
    mixed = _sigmoid(ga_ref[...].astype(F32)) * a + _sigmoid(gb_ref[...].astype(F32)) * b
    xn = x_ref[...] + gate1_ref[0] * _dot(mixed.astype(BF16), wo_ref[...])
    xo_ref[...] = xn
    ms = jnp.mean(xn * xn, axis=-1, keepdims=True)
    h2 = xn * lax.rsqrt(ms + NORM_EPS) * nw_ref[...]
    h2 = h2 * (1.0 + scale_ref[0]) + shift_ref[0]
    h2_ref[...] = _pack_bf16_pairs(h2)
    lg_ref[...] = jnp.dot(h2, wr_ref[...], preferred_element_type=F32, precision=HIGHEST)


def _merge_out(ya, yb, proj, x2, gate1, norm_w, shift, scale, wa, wb, wo, wr, seq):
    n, d = x2.shape
    tm = min(512, seq)
    per_b = seq // tm
    g0 = C_GATES // d
    tok = lambda j: pl.BlockSpec((tm, d), lambda i: (i, j))
    per_batch = pl.BlockSpec((1, 1, d), lambda i: (i // per_b, 0, 0))
    full = lambda r, c: pl.BlockSpec((r, c), lambda i: (0, 0))
    return pl.pallas_call(
        _merge_kernel,
        out_shape=(jax.ShapeDtypeStruct((n, d), F32),
                   jax.ShapeDtypeStruct((n, d // 2), jnp.uint32),
                   jax.ShapeDtypeStruct((n, LANES), F32)),
        grid=(n // tm,),
        in_specs=[tok(0), tok(0), tok(g0), tok(g0 + 1), tok(0), per_batch, full(1, d), per_batch, per_batch,
                  full(d, d), full(d, d), full(d, d), full(d, LANES)],
        out_specs=(tok(0), pl.BlockSpec((tm, d // 2), lambda i: (i, 0)),
                   pl.BlockSpec((tm, LANES), lambda i: (i, 0))),
        compiler_params=_cparams(("parallel",)),
        name="merge_out",
    )(ya, yb, proj, proj, x2, gate1, norm_w, shift, scale, wa, wb, wo, wr)


def _dispatch_kernel(dest_ref, h2_ref, xs_in_ref, xs_ref, sem, *, tile):
    del xs_in_ref

    def row_copy(t, k):
        return pltpu.make_async_copy(h2_ref.at[pl.ds(t, 1), :],
                                     xs_ref.at[pl.ds(dest_ref[t * TOP_K + k], 1), :], sem)

    def start(t, carry):
        for k in range(TOP_K):
            row_copy(t, k).start()
        return carry

    def wait(t, carry):
        for k in range(TOP_K):
            row_copy(t, k).wait()
        return carry

    lax.fori_loop(0, tile, start, 0)
    lax.fori_loop(0, tile, wait, 0)


def _dispatch(dest_flat, h2p, n_rows):
    n, half = h2p.shape
    tile = min(TOK_TILE, n)
    xs0 = jnp.zeros((n_rows, half), jnp.uint32)
    kern = functools.partial(_dispatch_kernel, tile=tile)
    return pl.pallas_call(
        kern,
        out_shape=jax.ShapeDtypeStruct((n_rows, half), jnp.uint32),
        grid=(n // tile,),
        in_specs=[pl.BlockSpec((tile * TOP_K,), lambda i: (i,), memory_space=pltpu.SMEM),
                  pl.BlockSpec((tile, half), lambda i: (i, 0)),
                  pl.BlockSpec(memory_space=pl.ANY)],
        out_specs=pl.BlockSpec(memory_space=pl.ANY),
        scratch_shapes=[pltpu.SemaphoreType.DMA],
        input_output_aliases={2: 0},
        compiler_params=_cparams(("arbitrary",)),
        name="moe_dispatch",
    )(dest_flat, h2p, xs0)


def _expert_kernel(be_ref, used_ref, x_ref, w1_ref, w3_ref, w2_ref, y_ref):
    live = pl.program_id(0) < used_ref[0]

    @pl.when(live)
    def _():
        x = _unpack_bf16_pairs(x_ref[...])
        a = _dot(x, w1_ref[0])
        g = _dot(x, w3_ref[0])
        y_ref[...] = _dot((_silu(a) * g).astype(BF16), w2_ref[0])

    @pl.when(jnp.logical_not(live))
    def _():
        y_ref[...] = jnp.zeros(y_ref.shape, F32)


def _experts(block_e, used, xs, w1, w3, w2):
    n_rows, half = xs.shape
    d = 2 * half
    n_blocks = n_rows // MOE_ROWS
    live = lambda i, be, used: jnp.minimum(i, used[0] - 1)
    return pl.pallas_call(
        _expert_kernel,
        out_shape=jax.ShapeDtypeStruct((n_rows, d), F32),
        grid_spec=pltpu.PrefetchScalarGridSpec(
            num_scalar_prefetch=2,
            grid=(n_blocks,),
            in_specs=[pl.BlockSpec((MOE_ROWS, half), lambda i, be, used: (live(i, be, used), 0)),
                      pl.BlockSpec((1, d, D_EXPERT), lambda i, be, used: (be[i], 0, 0)),
                      pl.BlockSpec((1, d, D_EXPERT), lambda i, be, used: (be[i], 0, 0)),
                      pl.BlockSpec((1, D_EXPERT, d), lambda i, be, used: (be[i], 0, 0))],
            out_specs=pl.BlockSpec((MOE_ROWS, d), lambda i, be, used: (i, 0))),
        compiler_params=_cparams(("arbitrary",)),
        name="moe_experts",
    )(block_e, used, xs, w1, w3, w2)


def _combine_kernel(dest_ref, ys_ref, x_ref, wt_ref, gate2_ref, fw_ref, o_ref, buf_ref, sem, *, tile, final):
    def row_copy(t, k):
        return pltpu.make_async_copy(ys_ref.at[pl.ds(dest_ref[t * TOP_K + k], 1), :],
                                     buf_ref.at[k, pl.ds(t, 1), :], sem)

    def start(t, carry):
        for k in range(TOP_K):
            row_copy(t, k).start()
        return carry

    def wait(t, carry):
        for k in range(TOP_K):
            row_copy(t, k).wait()
        return carry

    lax.fori_loop(0, tile, start, 0)
    lax.fori_loop(0, tile, wait, 0)
    wt = wt_ref[...]
    y = wt[:, 0:1] * buf_ref[0] + wt[:, 1:2] * buf_ref[1]
    xn = x_ref[...] + gate2_ref[0] * y
    if final:
        xn = xn * lax.rsqrt(jnp.mean(xn * xn, axis=-1, keepdims=True) + NORM_EPS) * fw_ref[...]
    o_ref[...] = xn


def _combine(dest_flat, ys, x2, wts, gate2, final_w, seq, final):
    n, d = x2.shape
    tile = min(TOK_TILE, seq)
    per_b = seq // tile
    kern = functools.partial(_combine_kernel, tile=tile, final=final)
    return pl.pallas_call(
        kern,
        out_shape=jax.ShapeDtypeStruct((n, d), F32),
        grid=(n // tile,),
        in_specs=[pl.BlockSpec((tile * TOP_K,), lambda i: (i,), memory_space=pltpu.SMEM),
                  pl.BlockSpec(memory_space=pl.ANY),
                  pl.BlockSpec((tile, d), lambda i: (i, 0)),
                  pl.BlockSpec((tile, TOP_K), lambda i: (i, 0)),
                  pl.BlockSpec((1, 1, d), lambda i: (i // per_b, 0, 0)),
                  pl.BlockSpec((1, d), lambda i: (0, 0))],
        out_specs=pl.BlockSpec((tile, d), lambda i: (i, 0)),
        scratch_shapes=[pltpu.VMEM((TOP_K, tile, d), F32), pltpu.SemaphoreType.DMA],
        compiler_params=_cparams(("arbitrary",)),
        name="moe_combine",
    )(dest_flat, ys, x2, wts, gate2, final_w)


def _route(logits, b_rg, b_re):
    n = logits.shape[0]
    gl = logits[:, 0:N_GROUPS] + b_rg
    el = logits[:, 8:8 + N_EXPERTS] + b_re
    gprob = jax.nn.softmax(gl, axis=-1)
    gsel = jnp.argmax(gprob, axis=-1)
    gweight = jnp.take_along_axis(gprob, gsel[:, None], axis=-1)
    el = el.reshape(n, N_GROUPS, EXPERTS_PER_GROUP)
    el = jnp.take_along_axis(el, gsel[:, None, None], axis=1)[:, 0]
    top_p, top_i = lax.top_k(jax.nn.softmax(el, axis=-1), TOP_K)
    weights = gweight * top_p / jnp.sum(top_p, axis=-1, keepdims=True)
    expert_ids = (gsel[:, None] * EXPERTS_PER_GROUP + top_i).astype(jnp.int32)
    flat_e = expert_ids.reshape(-1)
    onehot = (flat_e[:, None] == jnp.arange(N_EXPERTS, dtype=jnp.int32)[None, :]).astype(jnp.int32)
    csum = jnp.cumsum(onehot, axis=0)
    rank = jnp.take_along_axis(csum, flat_e[:, None], axis=1)[:, 0] - 1
    counts = csum[-1]
    padded = (counts + MOE_ROWS - 1) // MOE_ROWS * MOE_ROWS
    pad_end = jnp.cumsum(padded)
    pad_start = pad_end - padded
    dest = (pad_start[flat_e] + rank).astype(jnp.int32)
    n_blocks = (n * TOP_K) // MOE_ROWS + N_EXPERTS
    used = jnp.maximum(pad_end[-1] // MOE_ROWS, 1).astype(jnp.int32)
    blk = jnp.minimum(jnp.arange(n_blocks, dtype=jnp.int32), used - 1)
    block_e = jnp.minimum(jnp.searchsorted(pad_end, blk * MOE_ROWS, side='right'),
                          N_EXPERTS - 1).astype(jnp.int32)
    return dest, weights.astype(F32), block_e, used.reshape(1), n_blocks * MOE_ROWS


def kernel(x, c, ada_w, ada_b, norm1_w, w_in, conv_w, lambda_q1, lambda_k1, lambda_q2, lambda_k2,
           subln_w, a_log, dt_bias, gdn_norm_w, w_branch_a, w_branch_b, w_out, norm2_w,
           router_group_w, router_group_b, router_expert_w, router_expert_b,
           expert_w1, expert_w3, expert_w2, final_norm_w):
    bsz, seq, d = x.shape
    depth = ada_w.shape[0]
    n = bsz * seq
    mod = _ada_mod(c, ada_w, ada_b)
    x2 = x.reshape(n, d)
    n_beta = 7168
    for l in range(depth):
        lam_init = 0.8 - 0.6 * math.exp(-0.3 * l)
        m = mod[l].reshape(bsz, 1, 6, d)
        shift1, scale1, gate1, shift2, scale2, gate2 = (m[:, :, i] for i in range(6))
        w = w_in[l]
        w_main = jnp.concatenate([w[:, :n_beta], w[:, n_beta + 2 * GDN_HEADS:]], axis=1).astype(BF16)
        w_small = jnp.pad(w[:, n_beta:n_beta + 2 * GDN_HEADS], ((0, 0), (0, LANES - 2 * GDN_HEADS)))
        proj, small = _in_proj(x2, shift1, scale1, norm1_w[l].reshape(1, d), w_main, w_small, seq)
        proj3 = proj.reshape(bsz, seq, P_MAIN)
        lamv = jnp.pad(jnp.stack([lambda_q1[l], lambda_k1[l], lambda_q2[l], lambda_k2[l]]),
                       ((0, 0), (0, LANES - ATT_HEAD_DIM)))
        ya = _diff_attention(proj3, lamv, subln_w[l].reshape(1, LANES), lam_init)
        yb = _gated_deltanet(proj3, small.reshape(bsz, seq, LANES), conv_w[l], a_log[l], dt_bias[l],
                             gdn_norm_w[l].reshape(1, LANES))
        wr = jnp.zeros((d, LANES), F32)
        wr = wr.at[:, 0:N_GROUPS].set(router_group_w[l]).at[:, 8:8 + N_EXPERTS].set(router_expert_w[l])
        x2, h2p, logits = _merge_out(
            ya.reshape(n, d), yb.reshape(n, d), proj, x2, gate1, norm2_w[l].reshape(1, d), shift2, scale2,
            w_branch_a[l].astype(BF16), w_branch_b[l].astype(BF16), w_out[l].astype(BF16), wr, seq)
        dest, wts, block_e, used, n_rows = _route(logits, router_group_b[l], router_expert_b[l])
        xs = _dispatch(dest, h2p, n_rows)
        ys = _experts(block_e, used, xs, expert_w1[l].astype(BF16), expert_w3[l].astype(BF16),
                      expert_w2[l].astype(BF16))
        x2 = _combine(dest, ys, x2, wts, gate2, final_norm_w.reshape(1, d), seq, final=(l == depth - 1))
    return x2.reshape(bsz, seq, d)
```

```python
import functools
import math

import jax
import jax.numpy as jnp
import numpy as np
from jax import lax
from jax.experimental import pallas as pl
from jax.experimental.pallas import tpu as pltpu

F32 = jnp.float32
BF16 = jnp.bfloat16
HIGHEST = lax.Precision.HIGHEST

D_MODEL = 1024
LANES = 128
ATT_HEADS = 8
ATT_HEAD_DIM = 64
ALIBI_MAX_BIAS = 8.0
GDN_HEADS = 8
GDN_DK = 128
CONV_WIDTH = 4
GDN_CHUNK = 128
N_GROUPS = 4
EXPERTS_PER_GROUP = 8
N_EXPERTS = N_GROUPS * EXPERTS_PER_GROUP
TOP_K = 2
D_EXPERT = D_MODEL // 2
NORM_EPS = 1e-6
NEG_INF = -1e30

C_QA, C_KA, C_VA = 0, 1024, 2048
C_QB, C_KB, C_VB = 3072, 4096, 5120
C_ZB = 6144
C_GATES = 7168
P_MAIN = 9216

ATT_TQ = 1024
ATT_TK = 512
MOE_ROWS = 256
TOK_TILE = 256
VMEM_LIMIT = 56 * 1024 * 1024


def _cparams(sem):
    return pltpu.CompilerParams(dimension_semantics=sem, vmem_limit_bytes=VMEM_LIMIT)


def _sigmoid(x):
    return 1.0 / (1.0 + jnp.exp(-x))


def _silu(x):
    return x * _sigmoid(x)


def _dot(a, b):
    return jnp.dot(a, b, preferred_element_type=F32)


def _dot_nt(a, b):
    return lax.dot_general(a, b, (((1,), (1,)), ((), ())), preferred_element_type=F32)


def _ada_kernel(c_ref, w_ref, b_ref, o_ref):
    cond = _silu(c_ref[...])
    o_ref[0] = jnp.dot(cond, w_ref[0], preferred_element_type=F32, precision=HIGHEST) + b_ref[0]


def _ada_mod(c, ada_w, ada_b):
    depth, d, d6 = ada_w.shape
    bsz = c.shape[0]
    tn = 1536
    return pl.pallas_call(
        _ada_kernel,
        out_shape=jax.ShapeDtypeStruct((depth, bsz, d6), F32),
        grid=(depth, d6 // tn),
        in_specs=[pl.BlockSpec((bsz, d), lambda l, j: (0, 0)),
                  pl.BlockSpec((1, d, tn), lambda l, j: (l, 0, j)),
                  pl.BlockSpec((1, 1, tn), lambda l, j: (l, 0, j))],
        out_specs=pl.BlockSpec((1, bsz, tn), lambda l, j: (l, 0, j)),
        compiler_params=_cparams(("arbitrary", "arbitrary")),
        name="ada_mod",
    )(c, ada_w, ada_b.reshape(depth, 1, d6))


def _inproj_kernel(x_ref, shift_ref, scale_ref, nw_ref, w_ref, ws_ref, o_ref, os_ref, hn_ref):
    @pl.when(pl.program_id(1) == 0)
    def _():
        x = x_ref[...]
        ms = jnp.mean(x * x, axis=-1, keepdims=True)
        hn = x * lax.rsqrt(ms + NORM_EPS) * nw_ref[...]
        hn = hn * (1.0 + scale_ref[0]) + shift_ref[0]
        hn_ref[...] = hn.astype(BF16)
        os_ref[...] = jnp.dot(hn, ws_ref[...], preferred_element_type=F32, precision=HIGHEST)

    o_ref[...] = _dot(hn_ref[...], w_ref[...]).astype(BF16)


def _in_proj(x2, shift, scale, norm_w, w_main, w_small, seq):
    n, d = x2.shape
    tm = min(1024, seq)
    tn = 1024
    per_b = seq // tm
    return pl.pallas_call(
        _inproj_kernel,
        out_shape=(jax.ShapeDtypeStruct((n, P_MAIN), BF16), jax.ShapeDtypeStruct((n, LANES), F32)),
        grid=(n // tm, P_MAIN // tn),
        in_specs=[pl.BlockSpec((tm, d), lambda i, j: (i, 0)),
                  pl.BlockSpec((1, 1, d), lambda i, j: (i // per_b, 0, 0)),
                  pl.BlockSpec((1, 1, d), lambda i, j: (i // per_b, 0, 0)),
                  pl.BlockSpec((1, d), lambda i, j: (0, 0)),
                  pl.BlockSpec((d, tn), lambda i, j: (0, j)),
                  pl.BlockSpec((d, LANES), lambda i, j: (0, 0))],
        out_specs=(pl.BlockSpec((tm, tn), lambda i, j: (i, j)),
                   pl.BlockSpec((tm, LANES), lambda i, j: (i, 0))),
        scratch_shapes=[pltpu.VMEM((tm, d), BF16)],
        compiler_params=_cparams(("parallel", "arbitrary")),
        name="in_proj",
    )(x2, shift, scale, norm_w, w_main, w_small)


def _bf16_round(x):
    u = int(np.float32(x).view(np.uint32))
    u = (u + 0x7FFF + ((u >> 16) & 1)) & 0xFFFF0000
    return float(np.uint32(u).view(np.float32))


LOG2E = 1.4426950408889634
LOG2E_HI = _bf16_round(LOG2E)
LOG2E_LO = _bf16_round(LOG2E - LOG2E_HI)
N_FEAT = 7


def _feature_lanes(lane, base, values):
    out = jnp.zeros(lane.shape, F32)
    for i, val in enumerate(values):
        out = jnp.where(lane == base + i, val, out)
    return out


def _attn_kernel(slope_ref, q_ref, k_ref, v_ref, lamv_ref, subw_ref, o_ref,
                 ke_ref, ve_ref, qe_ref, m_ref, acc_ref, *, tq, tk, seq, lam_init):
    h = pl.program_id(1)
    qi = pl.program_id(2)
    slope = slope_ref[h]
    hd = ATT_HEAD_DIM
    prep = 512 if seq % 512 == 0 else seq

    @pl.when(qi == 0)
    def _():
        def fill(i, carry):
            r0 = pl.multiple_of(i * prep, prep)
            kf = k_ref[0, pl.ds(r0, prep), :].astype(F32)
            lane = lax.broadcasted_iota(jnp.int32, (prep, LANES), 1)
            j = r0 + lax.broadcasted_iota(jnp.int32, (prep, LANES), 0)
            f_lo = (j & (LANES - 1)).astype(F32) * slope
            f_hi = (j >> (LANES.bit_length() - 1)).astype(F32) * (slope * LANES)
            one = jnp.ones((prep, LANES), F32)
            vals = (f_lo, f_lo, f_hi, f_hi, one, one, one)
            ke_ref[0, pl.ds(r0, prep), :] = jnp.where(lane < hd, kf, _feature_lanes(lane, hd, vals)).astype(BF16)
            ke_ref[1, pl.ds(r0, prep), :] = jnp.where(lane >= hd, kf, _feature_lanes(lane, 0, vals)).astype(BF16)
            ve_ref[pl.ds(r0, prep), 0:LANES] = v_ref[0, pl.ds(r0, prep), :]
            ve_ref[pl.ds(r0, prep), LANES:2 * LANES] = jnp.ones((prep, LANES), BF16)
            return carry

        lax.fori_loop(0, seq // prep, fill, 0)

    q_start = qi * tq
    lane = lax.broadcasted_iota(jnp.int32, (tq, LANES), 1)
    qf = q_ref[0].astype(F32) * (hd ** -0.5 * LOG2E)
    cq = jnp.full((tq, LANES), 1.0, F32) * (slope * (-LOG2E) * q_start.astype(F32))
    c_hi = cq.astype(BF16).astype(F32)
    c_mid = (cq - c_hi).astype(BF16).astype(F32)
    c_lo = cq - c_hi - c_mid
    qvals = (LOG2E_HI, LOG2E_LO, LOG2E_HI, LOG2E_LO, c_hi, c_mid, c_lo)
    qe_ref[0] = jnp.where(lane < hd, qf, _feature_lanes(lane, hd, qvals)).astype(BF16)
    qe_ref[1] = jnp.where(lane >= hd, qf, _feature_lanes(lane, 0, qvals)).astype(BF16)
    m_ref[...] = jnp.full(m_ref.shape, NEG_INF, F32)
    acc_ref[...] = jnp.zeros(acc_ref.shape, F32)

    n_full = q_start // tk

    def blocks(js, masked):
        starts = [pl.multiple_of(j * tk, tk) for j in js]
        s_all = [[_dot_nt(qe_ref[mi], ke_ref[mi, pl.ds(k0, tk), :]) for mi in range(2)] for k0 in starts]
        if masked:
            r = lax.broadcasted_iota(jnp.int32, (tq, tk), 0)
            c = lax.broadcasted_iota(jnp.int32, (tq, tk), 1)
            rel = c - r
        for mi in range(2):
            m = m_ref[mi]
            acc = acc_ref[mi]
            for bi, k0 in enumerate(starts):
                s = s_all[bi][mi]
                if masked:
                    s = jnp.where(rel <= (q_start - k0), s, NEG_INF)
                m_new = jnp.maximum(m, jnp.max(s, axis=1, keepdims=True))
                alpha = jnp.exp2(m - m_new)
                p = jnp.exp2(s - jnp.tile(m_new, (1, tk // LANES))).astype(BF16)
                acc = jnp.tile(alpha, (1, 2)) * acc + _dot(p, ve_ref[pl.ds(k0, tk), :])
                m = m_new
            m_ref[mi] = m
            acc_ref[mi] = acc

    odd = n_full & 1

    @pl.when(odd == 1)
    def _():
        blocks([0], False)

    def loop_body(p, carry):
        blocks([odd + 2 * p, odd + 2 * p + 1], False)
        return carry

    lax.fori_loop(0, n_full >> 1, loop_body, 0)
    blocks([n_full + d for d in range(max(1, tq // tk))], True)

    lamv = lamv_ref[...]
    lam = (jnp.exp(jnp.sum(lamv[0:1] * lamv[1:2], axis=1, keepdims=True))
           - jnp.exp(jnp.sum(lamv[2:3] * lamv[3:4], axis=1, keepdims=True)) + lam_init)
    a1 = acc_ref[0]
    a2 = acc_ref[1]
    o = a1[:, 0:LANES] / a1[:, LANES:2 * LANES] - lam * (a2[:, 0:LANES] / a2[:, LANES:2 * LANES])
    y = o * lax.rsqrt(jnp.mean(o * o, axis=-1, keepdims=True) + NORM_EPS)
    o_ref[0] = (y * subw_ref[...] * (1.0 - lam_init)).astype(BF16)


def _diff_attention(proj3, lamv, subln_w, lam_init):
    bsz, seq, _ = proj3.shape
    tq = min(ATT_TQ, seq)
    tk = min(ATT_TK, seq)
    slopes = jnp.exp2(-ALIBI_MAX_BIAS * jnp.arange(1, ATT_HEADS + 1, dtype=F32) / ATT_HEADS)
    kern = functools.partial(_attn_kernel, tq=tq, tk=tk, seq=seq, lam_init=lam_init)
    nq = C_QA // LANES
    nk = C_KA // LANES
    nv = C_VA // LANES
    return pl.pallas_call(
        kern,
        out_shape=jax.ShapeDtypeStruct((bsz, seq, ATT_HEADS * LANES), BF16),
        grid_spec=pltpu.PrefetchScalarGridSpec(
            num_scalar_prefetch=1,
            grid=(bsz, ATT_HEADS, seq // tq),
            in_specs=[pl.BlockSpec((1, tq, LANES), lambda b, h, i, s: (b, i, nq + h)),
                      pl.BlockSpec((1, seq, LANES), lambda b, h, i, s: (b, 0, nk + h)),
                      pl.BlockSpec((1, seq, LANES), lambda b, h, i, s: (b, 0, nv + h)),
                      pl.BlockSpec((4, LANES), lambda b, h, i, s: (0, 0)),
                      pl.BlockSpec((1, LANES), lambda b, h, i, s: (0, 0))],
            out_specs=pl.BlockSpec((1, tq, LANES), lambda b, h, i, s: (b, i, h)),
            scratch_shapes=[pltpu.VMEM((2, seq, LANES), BF16),
                            pltpu.VMEM((seq, 2 * LANES), BF16),
                            pltpu.VMEM((2, tq, LANES), BF16),
                            pltpu.VMEM((2, tq, LANES), F32),
                            pltpu.VMEM((2, tq, 2 * LANES), F32)]),
        compiler_params=_cparams(("parallel", "arbitrary", "arbitrary")),
        name="diff_attn",
    )(slopes, proj3, proj3, proj3, lamv, subln_w)


def _gdn_kernel(q_ref, k_ref, v_ref, z_ref, small_ref, cwq_ref, cwk_ref, cwv_ref,
                alog_ref, dtb_ref, nw_ref, o_ref, xbuf_ref, state_ref, *, heads, chunk):
    c_idx = pl.program_id(1)
    width = heads * LANES
    pad = 8

    @pl.when(c_idx == 0)
    def _():
        xbuf_ref[:, 0:pad, :] = jnp.zeros((3, pad, width), F32)
        state_ref[...] = jnp.zeros(state_ref.shape, F32)

    def conv_silu(idx, raw_ref, cw_ref):
        xbuf_ref[idx, pad:pad + chunk, :] = raw_ref[0].astype(F32)
        cw = cw_ref[...]
        acc = xbuf_ref[idx, pad:pad + chunk, :] * cw[CONV_WIDTH - 1:CONV_WIDTH]
        for back in range(1, CONV_WIDTH):
            tap = CONV_WIDTH - 1 - back
            acc = acc + xbuf_ref[idx, pad - back:pad - back + chunk, :] * cw[tap:tap + 1]
        xbuf_ref[idx, 0:pad, :] = xbuf_ref[idx, chunk:chunk + pad, :]
        return _silu(acc)

    qc = conv_silu(0, q_ref, cwq_ref)
    kc = conv_silu(1, k_ref, cwk_ref)
    vc = conv_silu(2, v_ref, cwv_ref)

    small = small_ref[0]
    beta_all = _sigmoid(small)
    sp_in = small + dtb_ref[...]
    softplus = jnp.maximum(sp_in, 0.0) + jnp.log(1.0 + jnp.exp(-jnp.abs(sp_in)))
    g_all = -jnp.exp(alog_ref[...]) * softplus
    row = lax.broadcasted_iota(jnp.int32, (chunk, chunk), 0)
    col = lax.broadcasted_iota(jnp.int32, (chunk, chunk), 1)
    incl = col <= row
    strict = col < row
    tri = jnp.where(incl, 1.0, 0.0).astype(F32)
    gc_all = jnp.dot(tri, g_all, preferred_element_type=F32, precision=HIGHEST)
    eye = jnp.where(row == col, 1.0, 0.0).astype(F32)

    level_masks = []
    for k in range(chunk.bit_length() - 1):
        same = (row >> (k + 1)) == (col >> (k + 1))
        level_masks.append(same & (((row >> k) & 1) == 1) & (((col >> k) & 1) == 0))

    nw = nw_ref[...]
    hs = range(heads)
    sls = [slice(hh * LANES, (hh + 1) * LANES) for hh in hs]
    qn, kn, kb, gcs, egs, decay = [], [], [], [], [], []
    for hh in hs:
        qh, kh = qc[:, sls[hh]], kc[:, sls[hh]]
        qn.append(qh * lax.rsqrt(jnp.sum(qh * qh, axis=-1, keepdims=True) + NORM_EPS) * (GDN_DK ** -0.5))
        kn.append(kh * lax.rsqrt(jnp.sum(kh * kh, axis=-1, keepdims=True) + NORM_EPS))
        kb.append(kn[hh] * beta_all[:, hh:hh + 1])
        gc = gc_all[:, heads + hh:heads + hh + 1]
        gcs.append(gc)
        egs.append(jnp.exp(gc))
        g_col = jnp.broadcast_to(gc, (chunk, chunk))
        decay.append(jnp.where(incl, jnp.exp(jnp.where(incl, g_col - g_col.T, 0.0)), 0.0))
    kn16 = [kn[hh].astype(BF16) for hh in hs]
    kk = [_dot_nt(kb[hh].astype(BF16), kn16[hh]) for hh in hs]
    qk = [_dot_nt(qn[hh].astype(BF16), kn16[hh]) for hh in hs]
    low = [jnp.where(strict, kk[hh] * decay[hh], 0.0) for hh in hs]
    intra = [(qk[hh] * decay[hh]).astype(BF16) for hh in hs]
    inv = [eye - jnp.where(level_masks[0], low[hh], 0.0) for hh in hs]
    for m in level_masks[1:]:
        inv16 = [inv[hh].astype(BF16) for hh in hs]
        t = [_dot(jnp.where(m, low[hh], 0.0).astype(BF16), inv16[hh]).astype(BF16) for hh in hs]
        inv = [inv[hh] - _dot(inv16[hh], t[hh]) for hh in hs]
    rhs = [jnp.concatenate([vc[:, sls[hh]] * beta_all[:, hh:hh + 1], kb[hh] * egs[hh]], axis=1).astype(BF16)
           for hh in hs]
    uw = [_dot(inv[hh].astype(BF16), rhs[hh]) for hh in hs]
    state = [state_ref[hh] for hh in hs]
    state16 = [state[hh].astype(BF16) for hh in hs]
    v_new = [(uw[hh][:, 0:LANES] - _dot(uw[hh][:, LANES:2 * LANES].astype(BF16), state16[hh])).astype(BF16)
             for hh in hs]
    for hh in hs:
        g_last = gcs[hh][chunk - 1:chunk, :]
        kdec = kn[hh] * jnp.exp(g_last - gcs[hh])
        state_ref[hh] = state[hh] * jnp.exp(g_last) + _dot(kdec.T.astype(BF16), v_new[hh])
    for hh in hs:
        o = _dot((qn[hh] * egs[hh]).astype(BF16), state16[hh]) + _dot(intra[hh], v_new[hh])
        y = o * lax.rsqrt(jnp.mean(o * o, axis=-1, keepdims=True) + NORM_EPS) * nw
        zf = z_ref[0, :, sls[hh]].astype(F32)
        o_ref[0, :, sls[hh]] = (y * _silu(zf)).astype(BF16)


def _gated_deltanet(proj3, small3, conv_w, a_log, dt_bias, norm_w):
    bsz, seq, _ = proj3.shape
    heads = GDN_HEADS
    width = heads * LANES
    chunk = min(GDN_CHUNK, seq)
    zeros = jnp.zeros((LANES - 2 * heads,), F32)
    alog_row = jnp.concatenate([jnp.zeros((heads,), F32), a_log, zeros]).reshape(1, LANES)
    dtb_row = jnp.concatenate([jnp.zeros((heads,), F32), dt_bias, zeros]).reshape(1, LANES)
    kern = functools.partial(_gdn_kernel, heads=heads, chunk=chunk)
    blk = lambda off: pl.BlockSpec((1, chunk, width), lambda b, c: (b, c, off // width))
    cw = lambda j: pl.BlockSpec((CONV_WIDTH, width), lambda b, c: (0, j))
    row = pl.BlockSpec((1, LANES), lambda b, c: (0, 0))
    return pl.pallas_call(
        kern,
        out_shape=jax.ShapeDtypeStruct((bsz, seq, width), BF16),
        grid=(bsz, seq // chunk),
        in_specs=[blk(C_QB), blk(C_KB), blk(C_VB), blk(C_ZB),
                  pl.BlockSpec((1, chunk, LANES), lambda b, c: (b, c, 0)),
                  cw(0), cw(1), cw(2), row, row, row],
        out_specs=pl.BlockSpec((1, chunk, width), lambda b, c: (b, c, 0)),
        scratch_shapes=[pltpu.VMEM((3, chunk + 8, width), F32),
                        pltpu.VMEM((heads, GDN_DK, LANES), F32)],
        compiler_params=_cparams(("parallel", "arbitrary")),
        name="gated_deltanet",
    )(proj3, proj3, proj3, proj3, small3, conv_w, conv_w, conv_w, alog_row, dtb_row, norm_w)


def _pack_bf16_pairs(x):
    k = x.shape[1] // 2
    r = pltpu.bitcast(x.astype(BF16).astype(F32), jnp.uint32)
    return (r[:, k:] & jnp.uint32(0xFFFF0000)) | (r[:, :k] >> 16)


def _unpack_bf16_pairs(p):
    lo = pltpu.bitcast(p << 16, F32)
    hi = pltpu.bitcast(p & jnp.uint32(0xFFFF0000), F32)
    return jnp.concatenate([lo, hi], axis=1).astype(BF16)


def _merge_kernel(ya_ref, yb_ref, ga_ref, gb_ref, x_ref, gate1_ref, nw_ref, shift_ref, scale_ref,
                  wa_ref, wb_ref, wo_ref, wr_ref, xo_ref, h2_ref, lg_ref):
    a = _dot(ya_ref[...], wa_ref[...])
    b = _dot(yb_ref[...], wb_ref[...])
    mixed = _sigmoid(ga_ref[...].astype(F32)) * a + _sigmoid(gb_ref[...].astype(F32)) * b
    xn = x_ref[...] + gate1_ref[0] * _dot(mixed.astype(BF16), wo_ref[...])
    xo_ref[...] = xn
    ms = jnp.mean(xn * xn, axis=-1, keepdims=True)
    h2 = xn * lax.rsqrt(ms + NORM_EPS) * nw_ref[...]
    h2 = h2 * (1.0 + scale_ref[0]) + shift_ref[0]
    h2_ref[...] = _pack_bf16_pairs(h2)
    lg_ref[...] = jnp.dot(h2, wr_ref[...], preferred_element_type=F32, precision=HIGHEST)


def _merge_out(ya, yb, proj, x2, gate1, norm_w, shift, scale, wa, wb, wo, wr, seq):
    n, d = x2.shape
    tm = min(512, seq)
    per_b = seq // tm
    g0 = C_GATES // d
    tok = lambda j: pl.BlockSpec((tm, d), lambda i: (i, j))
    per_batch = pl.BlockSpec((1, 1, d), lambda i: (i // per_b, 0, 0))
    full = lambda r, c: pl.BlockSpec((r, c), lambda i: (0, 0))
    return pl.pallas_call(
        _merge_kernel,
        out_shape=(jax.ShapeDtypeStruct((n, d), F32),
                   jax.ShapeDtypeStruct((n, d // 2), jnp.uint32),
                   jax.ShapeDtypeStruct((n, LANES), F32)),
        grid=(n // tm,),
        in_specs=[tok(0), tok(0), tok(g0), tok(g0 + 1), tok(0), per_batch, full(1, d), per_batch, per_batch,
                  full(d, d), full(d, d), full(d, d), full(d, LANES)],
        out_specs=(tok(0), pl.BlockSpec((tm, d // 2), lambda i: (i, 0)),
                   pl.BlockSpec((tm, LANES), lambda i: (i, 0))),
        compiler_params=_cparams(("parallel",)),
        name="merge_out",
    )(ya, yb, proj, proj, x2, gate1, norm_w, shift, scale, wa, wb, wo, wr)


def _dispatch_kernel(dest_ref, h2_ref, xs_in_ref, xs_ref, sem, *, tile):
    del xs_in_ref

    def row_copy(t, k):
        return pltpu.make_async_copy(h2_ref.at[pl.ds(t, 1), :],
                                     xs_ref.at[pl.ds(dest_ref[t * TOP_K + k], 1), :], sem)

    def start(t, carry):
        for k in range(TOP_K):
            row_copy(t, k).start()
        return carry

    def wait(t, carry):
        for k in range(TOP_K):
            row_copy(t, k).wait()
        return carry

    lax.fori_loop(0, tile, start, 0)
    lax.fori_loop(0, tile, wait, 0)


def _dispatch(dest_flat, h2p, n_rows):
    n, half = h2p.shape
    tile = min(TOK_TILE, n)
    xs0 = jnp.zeros((n_rows, half), jnp.uint32)
    kern = functools.partial(_dispatch_kernel, tile=tile)
    return pl.pallas_call(
        kern,
        out_shape=jax.ShapeDtypeStruct((n_rows, half), jnp.uint32),
        grid=(n // tile,),
        in_specs=[pl.BlockSpec((tile * TOP_K,), lambda i: (i,), memory_space=pltpu.SMEM),
                  pl.BlockSpec((tile, half), lambda i: (i, 0)),
                  pl.BlockSpec(memory_space=pl.ANY)],
        out_specs=pl.BlockSpec(memory_space=pl.ANY),
        scratch_shapes=[pltpu.SemaphoreType.DMA],
        input_output_aliases={2: 0},
        compiler_params=_cparams(("arbitrary",)),
        name="moe_dispatch",
    )(dest_flat, h2p, xs0)


def _expert_kernel(be_ref, used_ref, x_ref, w1_ref, w3_ref, w2_ref, y_ref):
    live = pl.program_id(0) < used_ref[0]

    @pl.when(live)
    def _():
        x = _unpack_bf16_pairs(x_ref[...])
        a = _dot(x, w1_ref[0])
        g = _dot(x, w3_ref[0])
        y_ref[...] = _dot((_silu(a) * g).astype(BF16), w2_ref[0])

    @pl.when(jnp.logical_not(live))
    def _():
        y_ref[...] = jnp.zeros(y_ref.shape, F32)


def _experts(block_e, used, xs, w1, w3, w2):
    n_rows, half = xs.shape
    d = 2 * half
    n_blocks = n_rows // MOE_ROWS
    live = lambda i, be, used: jnp.minimum(i, used[0] - 1)
    return pl.pallas_call(
        _expert_kernel,
        out_shape=jax.ShapeDtypeStruct((n_rows, d), F32),
        grid_spec=pltpu.PrefetchScalarGridSpec(
            num_scalar_prefetch=2,
            grid=(n_blocks,),
            in_specs=[pl.BlockSpec((MOE_ROWS, half), lambda i, be, used: (live(i, be, used), 0)),
                      pl.BlockSpec((1, d, D_EXPERT), lambda i, be, used: (be[i], 0, 0)),
                      pl.BlockSpec((1, d, D_EXPERT), lambda i, be, used: (be[i], 0, 0)),
                      pl.BlockSpec((1, D_EXPERT, d), lambda i, be, used: (be[i], 0, 0))],
            out_specs=pl.BlockSpec((MOE_ROWS, d), lambda i, be, used: (i, 0))),
        compiler_params=_cparams(("arbitrary",)),
        name="moe_experts",
    )(block_e, used, xs, w1, w3, w2)


def _combine_kernel(dest_ref, ys_ref, x_ref, wt_ref, gate2_ref, fw_ref, o_ref, buf_ref, sem, *, tile, final):
    def row_copy(t, k):
        return pltpu.make_async_copy(ys_ref.at[pl.ds(dest_ref[t * TOP_K + k], 1), :],
                                     buf_ref.at[k, pl.ds(t, 1), :], sem)

    def start(t, carry):
        for k in range(TOP_K):
            row_copy(t, k).start()
        return carry

    def wait(t, carry):
        for k in range(TOP_K):
            row_copy(t, k).wait()
        return carry

    lax.fori_loop(0, tile, start, 0)
    lax.fori_loop(0, tile, wait, 0)
    wt = wt_ref[...]
    y = wt[:, 0:1] * buf_ref[0] + wt[:, 1:2] * buf_ref[1]
    xn = x_ref[...] + gate2_ref[0] * y
    if final:
        xn = xn * lax.rsqrt(jnp.mean(xn * xn, axis=-1, keepdims=True) + NORM_EPS) * fw_ref[...]
    o_ref[...] = xn


def _combine(dest_flat, ys, x2, wts, gate2, final_w, seq, final):
    n, d = x2.shape
    tile = min(TOK_TILE, seq)
    per_b = seq // tile
    kern = functools.partial(_combine_kernel, tile=tile, final=final)
    return pl.pallas_call(
        kern,
        out_shape=jax.ShapeDtypeStruct((n, d), F32),
        grid=(n // tile,),
        in_specs=[pl.BlockSpec((tile * TOP_K,), lambda i: (i,), memory_space=pltpu.SMEM),
                  pl.BlockSpec(memory_space=pl.ANY),
                  pl.BlockSpec((tile, d), lambda i: (i, 0)),
                  pl.BlockSpec((tile, TOP_K), lambda i: (i, 0)),
                  pl.BlockSpec((1, 1, d), lambda i: (i // per_b, 0, 0)),
                  pl.BlockSpec((1, d), lambda i: (0, 0))],
        out_specs=pl.BlockSpec((tile, d), lambda i: (i, 0)),
        scratch_shapes=[pltpu.VMEM((TOP_K, tile, d), F32), pltpu.SemaphoreType.DMA],
        compiler_params=_cparams(("arbitrary",)),
        name="moe_combine",
    )(dest_flat, ys, x2, wts, gate2, final_w)


def _route(logits, b_rg, b_re):
    n = logits.shape[0]
    gl = logits[:, 0:N_GROUPS] + b_rg
    el = logits[:, 8:8 + N_EXPERTS] + b_re
    gprob = jax.nn.softmax(gl, axis=-1)
    gsel = jnp.argmax(gprob, axis=-1)
    gweight = jnp.take_along_axis(gprob, gsel[:, None], axis=-1)
    el = el.reshape(n, N_GROUPS, EXPERTS_PER_GROUP)
    el = jnp.take_along_axis(el, gsel[:, None, None], axis=1)[:, 0]
    top_p, top_i = lax.top_k(jax.nn.softmax(el, axis=-1), TOP_K)
    weights = gweight * top_p / jnp.sum(top_p, axis=-1, keepdims=True)
    expert_ids = (gsel[:, None] * EXPERTS_PER_GROUP + top_i).astype(jnp.int32)
    flat_e = expert_ids.reshape(-1)
    onehot = (flat_e[:, None] == jnp.arange(N_EXPERTS, dtype=jnp.int32)[None, :]).astype(jnp.int32)
    csum = jnp.cumsum(onehot, axis=0)
    rank = jnp.take_along_axis(csum, flat_e[:, None], axis=1)[:, 0] - 1
    counts = csum[-1]
    padded = (counts + MOE_ROWS - 1) // MOE_ROWS * MOE_ROWS
    pad_end = jnp.cumsum(padded)
    pad_start = pad_end - padded
    dest = (pad_start[flat_e] + rank).astype(jnp.int32)
    n_blocks = (n * TOP_K) // MOE_ROWS + N_EXPERTS
    used = jnp.maximum(pad_end[-1] // MOE_ROWS, 1).astype(jnp.int32)
    blk = jnp.minimum(jnp.arange(n_blocks, dtype=jnp.int32), used - 1)
    block_e = jnp.minimum(jnp.searchsorted(pad_end, blk * MOE_ROWS, side='right'),
                          N_EXPERTS - 1).astype(jnp.int32)
    return dest, weights.astype(F32), block_e, used.reshape(1), n_blocks * MOE_ROWS


def kernel(x, c, ada_w, ada_b, norm1_w, w_in, conv_w, lambda_q1, lambda_k1, lambda_q2, lambda_k2,
           subln_w, a_log, dt_bias, gdn_norm_w, w_branch_a, w_branch_b, w_out, norm2_w,
           router_group_w, router_group_b, router_expert_w, router_expert_b,
           expert_w1, expert_w3, expert_w2, final_norm_w):
    bsz, seq, d = x.shape
    depth = ada_w.shape[0]
    n = bsz * seq
    mod = _ada_mod(c, ada_w, ada_b)
    x2 = x.reshape(n, d)
    n_beta = 7168
    for l in range(depth):
        lam_init = 0.8 - 0.6 * math.exp(-0.3 * l)
        m = mod[l].reshape(bsz, 1, 6, d)
        shift1, scale1, gate1, shift2, scale2, gate2 = (m[:, :, i] for i in range(6))
        w = w_in[l]
        w_main = jnp.concatenate([w[:, :n_beta], w[:, n_beta + 2 * GDN_HEADS:]], axis=1).astype(BF16)
        w_small = jnp.pad(w[:, n_beta:n_beta + 2 * GDN_HEADS], ((0, 0), (0, LANES - 2 * GDN_HEADS)))
        proj, small = _in_proj(x2, shift1, scale1, norm1_w[l].reshape(1, d), w_main, w_small, seq)
        proj3 = proj.reshape(bsz, seq, P_MAIN)
        lamv = jnp.pad(jnp.stack([lambda_q1[l], lambda_k1[l], lambda_q2[l], lambda_k2[l]]),
                       ((0, 0), (0, LANES - ATT_HEAD_DIM)))
        ya = _diff_attention(proj3, lamv, subln_w[l].reshape(1, LANES), lam_init)
        yb = _gated_deltanet(proj3, small.reshape(bsz, seq, LANES), conv_w[l], a_log[l], dt_bias[l],
                             gdn_norm_w[l].reshape(1, LANES))
        wr = jnp.zeros((d, LANES), F32)
        wr = wr.at[:, 0:N_GROUPS].set(router_group_w[l]).at[:, 8:8 + N_EXPERTS].set(router_expert_w[l])
        x2, h2p, logits = _merge_out(
            ya.reshape(n, d), yb.reshape(n, d), proj, x2, gate1, norm2_w[l].reshape(1, d), shift2, scale2,
            w_branch_a[l].astype(BF16), w_branch_b[l].astype(BF16), w_out[l].astype(BF16), wr, seq)
        dest, wts, block_e, used, n_rows = _route(logits, router_group_b[l], router_expert_b[l])
        xs = _dispatch(dest, h2p, n_rows)
        ys = _experts(block_e, used, xs, expert_w1[l].astype(BF16), expert_w3[l].astype(BF16),
                      expert_w2[l].astype(BF16))
        x2 = _combine(dest, ys, x2, wts, gate2, final_norm_w.reshape(1, d), seq, final=(l == depth - 1))
    return x2.reshape(bsz, seq, d)
```

```python
import functools
import math

import jax
import jax.numpy as jnp
import numpy as np
from jax import lax
from jax.experimental import pallas as pl
from jax.experimental.pallas import tpu as pltpu

F32 = jnp.float32
BF16 = jnp.bfloat16
HIGHEST = lax.Precision.HIGHEST

D_MODEL = 1024
LANES = 128
ATT_HEADS = 8
ATT_HEAD_DIM = 64
ALIBI_MAX_BIAS = 8.0
GDN_HEADS = 8
GDN_DK = 128
CONV_WIDTH = 4
GDN_CHUNK = 128
N_GROUPS = 4
EXPERTS_PER_GROUP = 8
N_EXPERTS = N_GROUPS * EXPERTS_PER_GROUP
TOP_K = 2
D_EXPERT = D_MODEL // 2
NORM_EPS = 1e-6
NEG_INF = -1e30

C_QA, C_KA, C_VA = 0, 1024, 2048
C_QB, C_KB, C_VB = 3072, 4096, 5120
C_ZB = 6144
C_GATES = 7168
P_MAIN = 9216

ATT_TQ = 1024
ATT_TK = 512
MERGE_SUB = 256
MOE_ROWS = 256
TOK_TILE = 256
DMA_UNROLL = 8
VMEM_LIMIT = 56 * 1024 * 1024


def _cparams(sem):
    return pltpu.CompilerParams(dimension_semantics=sem, vmem_limit_bytes=VMEM_LIMIT)


def _sigmoid(x):
    return 1.0 / (1.0 + jnp.exp(-x))


def _silu(x):
    return x * _sigmoid(x)


def _dot(a, b):
    return jnp.dot(a, b, preferred_element_type=F32)


def _split_hi_lo(w):
    hi = w.astype(BF16)
    lo = (w - hi.astype(F32)).astype(BF16)
    return jnp.concatenate([hi, lo], axis=1)


def _dot_split(x, w2_ref):
    hi = x.astype(BF16)
    lo = (x - hi.astype(F32)).astype(BF16)
    r = _dot(hi, w2_ref[...])
    return r[:, 0:LANES] + r[:, LANES:2 * LANES] + _dot(lo, w2_ref[:, 0:LANES])


def _dot_nt(a, b):
    return lax.dot_general(a, b, (((1,), (1,)), ((), ())), preferred_element_type=F32)


def _ada_kernel(c_ref, w_ref, b_ref, o_ref):
    cond = _silu(c_ref[...])
    o_ref[0] = jnp.dot(cond, w_ref[0], preferred_element_type=F32, precision=HIGHEST) + b_ref[0]


def _ada_mod(c, ada_w, ada_b):
    depth, d, d6 = ada_w.shape
    bsz = c.shape[0]
    tn = 1536
    return pl.pallas_call(
        _ada_kernel,
        out_shape=jax.ShapeDtypeStruct((depth, bsz, d6), F32),
        grid=(depth, d6 // tn),
        in_specs=[pl.BlockSpec((bsz, d), lambda l, j: (0, 0)),
                  pl.BlockSpec((1, d, tn), lambda l, j: (l, 0, j)),
                  pl.BlockSpec((1, 1, tn), lambda l, j: (l, 0, j))],
        out_specs=pl.BlockSpec((1, bsz, tn), lambda l, j: (l, 0, j)),
        compiler_params=_cparams(("arbitrary", "arbitrary")),
        name="ada_mod",
    )(c, ada_w, ada_b.reshape(depth, 1, d6))


def _inproj_kernel(x_ref, shift_ref, scale_ref, nw_ref, w_ref, ws_ref, o_ref, os_ref, hn_ref):
    @pl.when(pl.program_id(1) == 0)
    def _():
        x = x_ref[...]
        ms = jnp.mean(x * x, axis=-1, keepdims=True)
        hn = x * lax.rsqrt(ms + NORM_EPS) * nw_ref[...]
        hn = hn * (1.0 + scale_ref[0]) + shift_ref[0]
        hn_ref[...] = hn.astype(BF16)
        os_ref[...] = _dot_split(hn, ws_ref)

    o_ref[...] = _dot(hn_ref[...], w_ref[...]).astype(BF16)


def _in_proj(x2, shift, scale, norm_w, w_main, w_small, seq):
    n, d = x2.shape
    tm = min(1024, seq)
    tn = 1024
    per_b = seq // tm
    return pl.pallas_call(
        _inproj_kernel,
        out_shape=(jax.ShapeDtypeStruct((n, P_MAIN), BF16), jax.ShapeDtypeStruct((n, LANES), F32)),
        grid=(n // tm, P_MAIN // tn),
        in_specs=[pl.BlockSpec((tm, d), lambda i, j: (i, 0)),
                  pl.BlockSpec((1, 1, d), lambda i, j: (i // per_b, 0, 0)),
                  pl.BlockSpec((1, 1, d), lambda i, j: (i // per_b, 0, 0)),
                  pl.BlockSpec((1, d), lambda i, j: (0, 0)),
                  pl.BlockSpec((d, tn), lambda i, j: (0, j)),
                  pl.BlockSpec((d, 2 * LANES), lambda i, j: (0, 0))],
        out_specs=(pl.BlockSpec((tm, tn), lambda i, j: (i, j)),
                   pl.BlockSpec((tm, LANES), lambda i, j: (i, 0))),
        scratch_shapes=[pltpu.VMEM((tm, d), BF16)],
        compiler_params=_cparams(("parallel", "arbitrary")),
        name="in_proj",
    )(x2, shift, scale, norm_w, w_main, w_small)


def _bf16_round(x):
    u = int(np.float32(x).view(np.uint32))
    u = (u + 0x7FFF + ((u >> 16) & 1)) & 0xFFFF0000
    return float(np.uint32(u).view(np.float32))


LOG2E = 1.4426950408889634
LOG2E_HI = _bf16_round(LOG2E)
LOG2E_LO = _bf16_round(LOG2E - LOG2E_HI)
N_FEAT = 7


def _feature_lanes(lane, base, values):
    out = jnp.zeros(lane.shape, F32)
    for i, val in enumerate(values):
        out = jnp.where(lane == base + i, val, out)
    return out


def _attn_kernel(slope_ref, q_ref, k_ref, v_ref, lamv_ref, subw_ref, o_ref,
                 ke_ref, ve_ref, qe_ref, m_ref, acc_ref, *, tq, tk, seq, lam_init):
    h = pl.program_id(1)
    qi = pl.program_id(2)
    slope = slope_ref[h]
    hd = ATT_HEAD_DIM
    prep = 512 if seq % 512 == 0 else seq

    @pl.when(qi == 0)
    def _():
        def fill(i, carry):
            r0 = pl.multiple_of(i * prep, prep)
            kf = k_ref[0, pl.ds(r0, prep), :].astype(F32)
            lane = lax.broadcasted_iota(jnp.int32, (prep, LANES), 1)
            j = r0 + lax.broadcasted_iota(jnp.int32, (prep, LANES), 0)
            f_lo = (j & (LANES - 1)).astype(F32) * slope
            f_hi = (j >> (LANES.bit_length() - 1)).astype(F32) * (slope * LANES)
            one = jnp.ones((prep, LANES), F32)
            vals = (f_lo, f_lo, f_hi, f_hi, one, one, one)
            ke_ref[0, pl.ds(r0, prep), :] = jnp.where(lane < hd, kf, _feature_lanes(lane, hd, vals)).astype(BF16)
            ke_ref[1, pl.ds(r0, prep), :] = jnp.where(lane >= hd, kf, _feature_lanes(lane, 0, vals)).astype(BF16)
            ve_ref[pl.ds(r0, prep), 0:LANES] = v_ref[0, pl.ds(r0, prep), :]
            ve_ref[pl.ds(r0, prep), LANES:2 * LANES] = jnp.ones((prep, LANES), BF16)
            return carry

        lax.fori_loop(0, seq // prep, fill, 0)

    q_start = qi * tq
    lane = lax.broadcasted_iota(jnp.int32, (tq, LANES), 1)
    qf = q_ref[0].astype(F32) * (hd ** -0.5 * LOG2E)
    cq = jnp.full((tq, LANES), 1.0, F32) * (slope * (-LOG2E) * q_start.astype(F32))
    c_hi = cq.astype(BF16).astype(F32)
    c_mid = (cq - c_hi).astype(BF16).astype(F32)
    c_lo = cq - c_hi - c_mid
    qvals = (LOG2E_HI, LOG2E_LO, LOG2E_HI, LOG2E_LO, c_hi, c_mid, c_lo)
    qe_ref[0] = jnp.where(lane < hd, qf, _feature_lanes(lane, hd, qvals)).astype(BF16)
    qe_ref[1] = jnp.where(lane >= hd, qf, _feature_lanes(lane, 0, qvals)).astype(BF16)
    m_ref[...] = jnp.full(m_ref.shape, NEG_INF, F32)
    acc_ref[...] = jnp.zeros(acc_ref.shape, F32)

    n_full = q_start // tk

    def blocks(js, masked):
        starts = [pl.multiple_of(j * tk, tk) for j in js]
        s_all = [[_dot_nt(qe_ref[mi], ke_ref[mi, pl.ds(k0, tk), :]) for mi in range(2)] for k0 in starts]
        if masked:
            r = lax.broadcasted_iota(jnp.int32, (tq, tk), 0)
            c = lax.broadcasted_iota(jnp.int32, (tq, tk), 1)
            rel = c - r
        for mi in range(2):
            m = m_ref[mi]
            acc = acc_ref[mi]
            for bi, k0 in enumerate(starts):
                s = s_all[bi][mi]
                if masked:
                    s = jnp.where(rel <= (q_start - k0), s, NEG_INF)
                m_new = jnp.maximum(m, jnp.max(s, axis=1, keepdims=True))
                alpha = jnp.exp2(m - m_new)
                p = jnp.exp2(s - jnp.tile(m_new, (1, tk // LANES))).astype(BF16)
                acc = jnp.tile(alpha, (1, 2)) * acc + _dot(p, ve_ref[pl.ds(k0, tk), :])
                m = m_new
            m_ref[mi] = m
            acc_ref[mi] = acc

    odd = n_full & 1

    @pl.when(odd == 1)
    def _():
        blocks([0], False)

    def loop_body(p, carry):
        blocks([odd + 2 * p, odd + 2 * p + 1], False)
        return carry

    lax.fori_loop(0, n_full >> 1, loop_body, 0)
    blocks([n_full + d for d in range(max(1, tq // tk))], True)

    lamv = lamv_ref[...]
    lam = (jnp.exp(jnp.sum(lamv[0:1] * lamv[1:2], axis=1, keepdims=True))
           - jnp.exp(jnp.sum(lamv[2:3] * lamv[3:4], axis=1, keepdims=True)) + lam_init)
    a1 = acc_ref[0]
    a2 = acc_ref[1]
    o = a1[:, 0:LANES] / a1[:, LANES:2 * LANES] - lam * (a2[:, 0:LANES] / a2[:, LANES:2 * LANES])
    y = o * lax.rsqrt(jnp.mean(o * o, axis=-1, keepdims=True) + NORM_EPS)
    o_ref[0] = (y * subw_ref[...] * (1.0 - lam_init)).astype(BF16)


def _diff_attention(proj3, lamv, subln_w, lam_init):
    bsz, seq, _ = proj3.shape
    tq = min(ATT_TQ, seq)
    tk = min(ATT_TK, seq)
    slopes = jnp.exp2(-ALIBI_MAX_BIAS * jnp.arange(1, ATT_HEADS + 1, dtype=F32) / ATT_HEADS)
    kern = functools.partial(_attn_kernel, tq=tq, tk=tk, seq=seq, lam_init=lam_init)
    nq = C_QA // LANES
    nk = C_KA // LANES
    nv = C_VA // LANES
    return pl.pallas_call(
        kern,
        out_shape=jax.ShapeDtypeStruct((bsz, seq, ATT_HEADS * LANES), BF16),
        grid_spec=pltpu.PrefetchScalarGridSpec(
            num_scalar_prefetch=1,
            grid=(bsz, ATT_HEADS, seq // tq),
            in_specs=[pl.BlockSpec((1, tq, LANES), lambda b, h, i, s: (b, i, nq + h)),
                      pl.BlockSpec((1, seq, LANES), lambda b, h, i, s: (b, 0, nk + h)),
                      pl.BlockSpec((1, seq, LANES), lambda b, h, i, s: (b, 0, nv + h)),
                      pl.BlockSpec((4, LANES), lambda b, h, i, s: (0, 0)),
                      pl.BlockSpec((1, LANES), lambda b, h, i, s: (0, 0))],
            out_specs=pl.BlockSpec((1, tq, LANES), lambda b, h, i, s: (b, i, h)),
            scratch_shapes=[pltpu.VMEM((2, seq, LANES), BF16),
                            pltpu.VMEM((seq, 2 * LANES), BF16),
                            pltpu.VMEM((2, tq, LANES), BF16),
                            pltpu.VMEM((2, tq, LANES), F32),
                            pltpu.VMEM((2, tq, 2 * LANES), F32)]),
        compiler_params=_cparams(("parallel", "arbitrary", "arbitrary")),
        name="diff_attn",
    )(slopes, proj3, proj3, proj3, lamv, subln_w)


def _gdn_kernel(q_ref, k_ref, v_ref, z_ref, small_ref, cwq_ref, cwk_ref, cwv_ref,
                alog_ref, dtb_ref, nw_ref, o_ref, xbuf_ref, state_ref, *, heads, chunk):
    c_idx = pl.program_id(1)
    width = heads * LANES
    pad = 8

    @pl.when(c_idx == 0)
    def _():
        xbuf_ref[:, 0:pad, :] = jnp.zeros((3, pad, width), F32)
        state_ref[...] = jnp.zeros(state_ref.shape, F32)

    def conv_silu(idx, raw_ref, cw_ref):
        xbuf_ref[idx, pad:pad + chunk, :] = raw_ref[0].astype(F32)
        cw = cw_ref[...]
        acc = xbuf_ref[idx, pad:pad + chunk, :] * cw[CONV_WIDTH - 1:CONV_WIDTH]
        for back in range(1, CONV_WIDTH):
            tap = CONV_WIDTH - 1 - back
            acc = acc + xbuf_ref[idx, pad - back:pad - back + chunk, :] * cw[tap:tap + 1]
        xbuf_ref[idx, 0:pad, :] = xbuf_ref[idx, chunk:chunk + pad, :]
        return _silu(acc)

    qc = conv_silu(0, q_ref, cwq_ref)
    kc = conv_silu(1, k_ref, cwk_ref)
    vc = conv_silu(2, v_ref, cwv_ref)

    small = small_ref[0]
    beta_all = _sigmoid(small)
    sp_in = small + dtb_ref[...]
    softplus = jnp.maximum(sp_in, 0.0) + jnp.log(1.0 + jnp.exp(-jnp.abs(sp_in)))
    g_all = -jnp.exp(alog_ref[...]) * softplus
    row = lax.broadcasted_iota(jnp.int32, (chunk, chunk), 0)
    col = lax.broadcasted_iota(jnp.int32, (chunk, chunk), 1)
    incl = col <= row
    strict = col < row
    tri = jnp.where(incl, 1.0, 0.0).astype(F32)
    gc_all = jnp.dot(tri, g_all, preferred_element_type=F32, precision=HIGHEST)
    eye = jnp.where(row == col, 1.0, 0.0).astype(F32)

    level_masks = []
    for k in range(chunk.bit_length() - 1):
        same = (row >> (k + 1)) == (col >> (k + 1))
        level_masks.append(same & (((row >> k) & 1) == 1) & (((col >> k) & 1) == 0))

    nw = nw_ref[...]
    hs = range(heads)
    sls = [slice(hh * LANES, (hh + 1) * LANES) for hh in hs]
    qn, kn, kb, gcs, egs, decay = [], [], [], [], [], []
    for hh in hs:
        qh, kh = qc[:, sls[hh]], kc[:, sls[hh]]
        qn.append(qh * lax.rsqrt(jnp.sum(qh * qh, axis=-1, keepdims=True) + NORM_EPS) * (GDN_DK ** -0.5))
        kn.append(kh * lax.rsqrt(jnp.sum(kh * kh, axis=-1, keepdims=True) + NORM_EPS))
        kb.append(kn[hh] * beta_all[:, hh:hh + 1])
        gc = gc_all[:, heads + hh:heads + hh + 1]
        gcs.append(gc)
        egs.append(jnp.exp(gc))
        g_col = jnp.broadcast_to(gc, (chunk, chunk))
        decay.append(jnp.where(incl, jnp.exp(jnp.where(incl, g_col - g_col.T, 0.0)), 0.0))
    kn16 = [kn[hh].astype(BF16) for hh in hs]
    kk = [_dot_nt(kb[hh].astype(BF16), kn16[hh]) for hh in hs]
    qk = [_dot_nt(qn[hh].astype(BF16), kn16[hh]) for hh in hs]
    low = [jnp.where(strict, kk[hh] * decay[hh], 0.0) for hh in hs]
    intra = [(qk[hh] * decay[hh]).astype(BF16) for hh in hs]
    inv = [eye - jnp.where(level_masks[0], low[hh], 0.0) for hh in hs]
    for m in level_masks[1:]:
        inv16 = [inv[hh].astype(BF16) for hh in hs]
        t = [_dot(jnp.where(m, low[hh], 0.0).astype(BF16), inv16[hh]).astype(BF16) for hh in hs]
        inv = [inv[hh] - _dot(inv16[hh], t[hh]) for hh in hs]
    rhs = [jnp.concatenate([vc[:, sls[hh]] * beta_all[:, hh:hh + 1], kb[hh] * egs[hh]], axis=1).astype(BF16)
           for hh in hs]
    uw = [_dot(inv[hh].astype(BF16), rhs[hh]) for hh in hs]
    state = [state_ref[hh] for hh in hs]
    state16 = [state[hh].astype(BF16) for hh in hs]
    v_new = [(uw[hh][:, 0:LANES] - _dot(uw[hh][:, LANES:2 * LANES].astype(BF16), state16[hh])).astype(BF16)
             for hh in hs]
    for hh in hs:
        g_last = gcs[hh][chunk - 1:chunk, :]
        kdec = kn[hh] * jnp.exp(g_last - gcs[hh])
        state_ref[hh] = state[hh] * jnp.exp(g_last) + _dot(kdec.T.astype(BF16), v_new[hh])
    for hh in hs:
        o = _dot((qn[hh] * egs[hh]).astype(BF16), state16[hh]) + _dot(intra[hh], v_new[hh])
        y = o * lax.rsqrt(jnp.mean(o * o, axis=-1, keepdims=True) + NORM_EPS) * nw
        zf = z_ref[0, :, sls[hh]].astype(F32)
        o_ref[0, :, sls[hh]] = (y * _silu(zf)).astype(BF16)


def _gated_deltanet(proj3, small3, conv_w, a_log, dt_bias, norm_w):
    bsz, seq, _ = proj3.shape
    heads = GDN_HEADS
    width = heads * LANES
    chunk = min(GDN_CHUNK, seq)
    zeros = jnp.zeros((LANES - 2 * heads,), F32)
    alog_row = jnp.concatenate([jnp.zeros((heads,), F32), a_log, zeros]).reshape(1, LANES)
    dtb_row = jnp.concatenate([jnp.zeros((heads,), F32), dt_bias, zeros]).reshape(1, LANES)
    kern = functools.partial(_gdn_kernel, heads=heads, chunk=chunk)
    blk = lambda off: pl.BlockSpec((1, chunk, width), lambda b, c: (b, c, off // width))
    cw = lambda j: pl.BlockSpec((CONV_WIDTH, width), lambda b, c: (0, j))
    row = pl.BlockSpec((1, LANES), lambda b, c: (0, 0))
    return pl.pallas_call(
        kern,
        out_shape=jax.ShapeDtypeStruct((bsz, seq, width), BF16),
        grid=(bsz, seq // chunk),
        in_specs=[blk(C_QB), blk(C_KB), blk(C_VB), blk(C_ZB),
                  pl.BlockSpec((1, chunk, LANES), lambda b, c: (b, c, 0)),
                  cw(0), cw(1), cw(2), row, row, row],
        out_specs=pl.BlockSpec((1, chunk, width), lambda b, c: (b, c, 0)),
        scratch_shapes=[pltpu.VMEM((3, chunk + 8, width), F32),
                        pltpu.VMEM((heads, GDN_DK, LANES), F32)],
        compiler_params=_cparams(("parallel", "arbitrary")),
        name="gated_deltanet",
    )(proj3, proj3, proj3, proj3, small3, conv_w, conv_w, conv_w, alog_row, dtb_row, norm_w)


def _pack_bf16_pairs(x):
    k = x.shape[1] // 2
    r = pltpu.bitcast(x.astype(BF16).astype(F32), jnp.uint32)
    return (r[:, k:] & jnp.uint32(0xFFFF0000)) | (r[:, :k] >> 16)


def _unpack_bf16_pairs(p):
    lo = pltpu.bitcast(p << 16, F32)
    hi = pltpu.bitcast(p & jnp.uint32(0xFFFF0000), F32)
    return jnp.concatenate([lo, hi], axis=1).astype(BF16)


def _merge_kernel(ya_ref, yb_ref, ga_ref, gb_ref, x_ref, gate1_ref, nw_ref, shift_ref, scale_ref,
                  wa_ref, wb_ref, wo_ref, wr_ref, xo_ref, h2_ref, lg_ref):
    tm = x_ref.shape[0]
    sub = min(MERGE_SUB, tm)
    rows = [slice(i * sub, (i + 1) * sub) for i in range(tm // sub)]
    a = [_dot(ya_ref[r, :], wa_ref[...]) for r in rows]
    b = [_dot(yb_ref[r, :], wb_ref[...]) for r in rows]
    mixed = [(_sigmoid(ga_ref[r, :].astype(F32)) * a[i] + _sigmoid(gb_ref[r, :].astype(F32)) * b[i]).astype(BF16)
             for i, r in enumerate(rows)]
    upd = [_dot(m, wo_ref[...]) for m in mixed]
    for i, r in enumerate(rows):
        xn = x_ref[r, :] + gate1_ref[0] * upd[i]
        xo_ref[r, :] = xn
        ms = jnp.mean(xn * xn, axis=-1, keepdims=True)
        h2 = xn * lax.rsqrt(ms + NORM_EPS) * nw_ref[...]
        h2 = h2 * (1.0 + scale_ref[0]) + shift_ref[0]
        h2_ref[r, :] = _pack_bf16_pairs(h2)
        lg_ref[r, :] = _dot_split(h2, wr_ref)


def _merge_out(ya, yb, proj, x2, gate1, norm_w, shift, scale, wa, wb, wo, wr, seq):
    n, d = x2.shape
    tm = min(512, seq)
    per_b = seq // tm
    g0 = C_GATES // d
    tok = lambda j: pl.BlockSpec((tm, d), lambda i: (i, j))
    per_batch = pl.BlockSpec((1, 1, d), lambda i: (i // per_b, 0, 0))
    full = lambda r, c: pl.BlockSpec((r, c), lambda i: (0, 0))
    return pl.pallas_call(
        _merge_kernel,
        out_shape=(jax.ShapeDtypeStruct((n, d), F32),
                   jax.ShapeDtypeStruct((n, d // 2), jnp.uint32),
                   jax.ShapeDtypeStruct((n, LANES), F32)),
        grid=(n // tm,),
        in_specs=[tok(0), tok(0), tok(g0), tok(g0 + 1), tok(0), per_batch, full(1, d), per_batch, per_batch,
                  full(d, d), full(d, d), full(d, d), full(d, 2 * LANES)],
        out_specs=(tok(0), pl.BlockSpec((tm, d // 2), lambda i: (i, 0)),
                   pl.BlockSpec((tm, LANES), lambda i: (i, 0))),
        compiler_params=_cparams(("parallel",)),
        name="merge_out",
    )(ya, yb, proj, proj, x2, gate1, norm_w, shift, scale, wa, wb, wo, wr)


def _dispatch_kernel(dest_ref, h2_ref, xs_in_ref, xs_ref, sems, *, tile):
    del xs_in_ref
    i = pl.program_id(0)
    slot = i & 1
    tok0 = i * tile

    def row_copy(t, k, s):
        return pltpu.make_async_copy(h2_ref.at[pl.ds(tok0 + t, 1), :],
                                     xs_ref.at[pl.ds(dest_ref[t * TOP_K + k], 1), :], sems.at[s])

    def start(t, carry):
        for k in range(TOP_K):
            row_copy(t, k, slot).start()
        return carry

    def drain(s):
        def wait(t, carry):
            for k in range(TOP_K):
                row_copy(t, k, s).wait()
            return carry

        lax.fori_loop(0, tile, wait, 0, unroll=DMA_UNROLL)

    lax.fori_loop(0, tile, start, 0, unroll=DMA_UNROLL)

    @pl.when(i > 0)
    def _():
        drain(1 - slot)

    @pl.when(i == pl.num_programs(0) - 1)
    def _():
        drain(slot)


def _dispatch(dest_flat, h2p, n_rows):
    n, half = h2p.shape
    tile = min(TOK_TILE, n)
    xs0 = jnp.zeros((n_rows, half), jnp.uint32)
    kern = functools.partial(_dispatch_kernel, tile=tile)
    return pl.pallas_call(
        kern,
        out_shape=jax.ShapeDtypeStruct((n_rows, half), jnp.uint32),
        grid=(n // tile,),
        in_specs=[pl.BlockSpec((tile * TOP_K,), lambda i: (i,), memory_space=pltpu.SMEM),
                  pl.BlockSpec(memory_space=pl.ANY),
                  pl.BlockSpec(memory_space=pl.ANY)],
        out_specs=pl.BlockSpec(memory_space=pl.ANY),
        scratch_shapes=[pltpu.SemaphoreType.DMA((2,))],
        input_output_aliases={2: 0},
        compiler_params=_cparams(("arbitrary",)),
        name="moe_dispatch",
    )(dest_flat, h2p, xs0)


def _expert_kernel(be_ref, used_ref, x_ref, w1_ref, w3_ref, w2_ref, y_ref):
    live = pl.program_id(0) < used_ref[0]

    @pl.when(live)
    def _():
        x = _unpack_bf16_pairs(x_ref[...])
        a = _dot(x, w1_ref[0])
        g = _dot(x, w3_ref[0])
        y_ref[...] = _dot((_silu(a) * g).astype(BF16), w2_ref[0])

    @pl.when(jnp.logical_not(live))
    def _():
        y_ref[...] = jnp.zeros(y_ref.shape, F32)


def _experts(block_e, used, xs, w1, w3, w2):
    n_rows, half = xs.shape
    d = 2 * half
    n_blocks = n_rows // MOE_ROWS
    live = lambda i, be, used: jnp.minimum(i, used[0] - 1)
    return pl.pallas_call(
        _expert_kernel,
        out_shape=jax.ShapeDtypeStruct((n_rows, d), F32),
        grid_spec=pltpu.PrefetchScalarGridSpec(
            num_scalar_prefetch=2,
            grid=(n_blocks,),
            in_specs=[pl.BlockSpec((MOE_ROWS, half), lambda i, be, used: (live(i, be, used), 0)),
                      pl.BlockSpec((1, d, D_EXPERT), lambda i, be, used: (be[i], 0, 0)),
                      pl.BlockSpec((1, d, D_EXPERT), lambda i, be, used: (be[i], 0, 0)),
                      pl.BlockSpec((1, D_EXPERT, d), lambda i, be, used: (be[i], 0, 0))],
            out_specs=pl.BlockSpec((MOE_ROWS, d), lambda i, be, used: (i, 0))),
        compiler_params=_cparams(("arbitrary",)),
        name="moe_experts",
    )(block_e, used, xs, w1, w3, w2)


def _combine_kernel(dcur_ref, dnxt_ref, ys_ref, x_ref, wt_ref, gate2_ref, fw_ref, o_ref, buf_ref, sems,
                    *, tile, final):
    i = pl.program_id(0)
    slot = i & 1

    def row_copy(d_ref, t, k, s):
        return pltpu.make_async_copy(ys_ref.at[pl.ds(d_ref[t * TOP_K + k], 1), :],
                                     buf_ref.at[s, k, pl.ds(t, 1), :], sems.at[s])

    def gather(d_ref, s):
        def start(t, carry):
            for k in range(TOP_K):
                row_copy(d_ref, t, k, s).start()
            return carry

        lax.fori_loop(0, tile, start, 0, unroll=DMA_UNROLL)

    @pl.when(i == 0)
    def _():
        gather(dcur_ref, 0)

    @pl.when(i < pl.num_programs(0) - 1)
    def _():
        gather(dnxt_ref, 1 - slot)

    def wait(t, carry):
        for k in range(TOP_K):
            row_copy(dcur_ref, t, k, slot).wait()
        return carry

    lax.fori_loop(0, tile, wait, 0, unroll=DMA_UNROLL)
    wt = wt_ref[...]
    y = wt[:, 0:1] * buf_ref[slot, 0] + wt[:, 1:2] * buf_ref[slot, 1]
    xn = x_ref[...] + gate2_ref[0] * y
    if final:
        xn = xn * lax.rsqrt(jnp.mean(xn * xn, axis=-1, keepdims=True) + NORM_EPS) * fw_ref[...]
    o_ref[...] = xn


def _combine(dest_flat, ys, x2, wts, gate2, final_w, seq, final):
    n, d = x2.shape
    tile = min(TOK_TILE, seq)
    per_b = seq // tile
    steps = n // tile
    kern = functools.partial(_combine_kernel, tile=tile, final=final)
    return pl.pallas_call(
        kern,
        out_shape=jax.ShapeDtypeStruct((n, d), F32),
        grid=(steps,),
        in_specs=[pl.BlockSpec((tile * TOP_K,), lambda i: (i,), memory_space=pltpu.SMEM),
                  pl.BlockSpec((tile * TOP_K,), lambda i: (jnp.minimum(i + 1, steps - 1),),
                               memory_space=pltpu.SMEM),
                  pl.BlockSpec(memory_space=pl.ANY),
                  pl.BlockSpec((tile, d), lambda i: (i, 0)),
                  pl.BlockSpec((tile, TOP_K), lambda i: (i, 0)),
                  pl.BlockSpec((1, 1, d), lambda i: (i // per_b, 0, 0)),
                  pl.BlockSpec((1, d), lambda i: (0, 0))],
        out_specs=pl.BlockSpec((tile, d), lambda i: (i, 0)),
        scratch_shapes=[pltpu.VMEM((2, TOP_K, tile, d), F32), pltpu.SemaphoreType.DMA((2,))],
        compiler_params=_cparams(("arbitrary",)),
        name="moe_combine",
    )(dest_flat, dest_flat, ys, x2, wts, gate2, final_w)


def _route(logits, b_rg, b_re):
    n = logits.shape[0]
    gl = logits[:, 0:N_GROUPS] + b_rg
    el = logits[:, 8:8 + N_EXPERTS] + b_re
    gprob = jax.nn.softmax(gl, axis=-1)
    gsel = jnp.argmax(gprob, axis=-1)
    gweight = jnp.take_along_axis(gprob, gsel[:, None], axis=-1)
    el = el.reshape(n, N_GROUPS, EXPERTS_PER_GROUP)
    el = jnp.take_along_axis(el, gsel[:, None, None], axis=1)[:, 0]
    top_p, top_i = lax.top_k(jax.nn.softmax(el, axis=-1), TOP_K)
    weights = gweight * top_p / jnp.sum(top_p, axis=-1, keepdims=True)
    expert_ids = (gsel[:, None] * EXPERTS_PER_GROUP + top_i).astype(jnp.int32)
    flat_e = expert_ids.reshape(-1)
    onehot = (flat_e[:, None] == jnp.arange(N_EXPERTS, dtype=jnp.int32)[None, :]).astype(jnp.int32)
    csum = jnp.cumsum(onehot, axis=0)
    rank = jnp.take_along_axis(csum, flat_e[:, None], axis=1)[:, 0] - 1
    counts = csum[-1]
    padded = (counts + MOE_ROWS - 1) // MOE_ROWS * MOE_ROWS
    pad_end = jnp.cumsum(padded)
    pad_start = pad_end - padded
    dest = (pad_start[flat_e] + rank).astype(jnp.int32)
    n_blocks = (n * TOP_K) // MOE_ROWS + N_EXPERTS
    used = jnp.maximum(pad_end[-1] // MOE_ROWS, 1).astype(jnp.int32)
    blk = jnp.minimum(jnp.arange(n_blocks, dtype=jnp.int32), used - 1)
    block_e = jnp.minimum(jnp.searchsorted(pad_end, blk * MOE_ROWS, side='right'),
                          N_EXPERTS - 1).astype(jnp.int32)
    return dest, weights.astype(F32), block_e, used.reshape(1), n_blocks * MOE_ROWS


def kernel(x, c, ada_w, ada_b, norm1_w, w_in, conv_w, lambda_q1, lambda_k1, lambda_q2, lambda_k2,
           subln_w, a_log, dt_bias, gdn_norm_w, w_branch_a, w_branch_b, w_out, norm2_w,
           router_group_w, router_group_b, router_expert_w, router_expert_b,
           expert_w1, expert_w3, expert_w2, final_norm_w):
    bsz, seq, d = x.shape
    depth = ada_w.shape[0]
    n = bsz * seq
    mod = _ada_mod(c, ada_w, ada_b)
    x2 = x.reshape(n, d)
    n_beta = 7168
    for l in range(depth):
        lam_init = 0.8 - 0.6 * math.exp(-0.3 * l)
        m = mod[l].reshape(bsz, 1, 6, d)
        shift1, scale1, gate1, shift2, scale2, gate2 = (m[:, :, i] for i in range(6))
        w = w_in[l]
        w_main = jnp.concatenate([w[:, :n_beta], w[:, n_beta + 2 * GDN_HEADS:]], axis=1).astype(BF16)
        w_small = _split_hi_lo(jnp.pad(w[:, n_beta:n_beta + 2 * GDN_HEADS], ((0, 0), (0, LANES - 2 * GDN_HEADS))))
        proj, small = _in_proj(x2, shift1, scale1, norm1_w[l].reshape(1, d), w_main, w_small, seq)
        proj3 = proj.reshape(bsz, seq, P_MAIN)
        lamv = jnp.pad(jnp.stack([lambda_q1[l], lambda_k1[l], lambda_q2[l], lambda_k2[l]]),
                       ((0, 0), (0, LANES - ATT_HEAD_DIM)))
        ya = _diff_attention(proj3, lamv, subln_w[l].reshape(1, LANES), lam_init)
        yb = _gated_deltanet(proj3, small.reshape(bsz, seq, LANES), conv_w[l], a_log[l], dt_bias[l],
                             gdn_norm_w[l].reshape(1, LANES))
        wr = jnp.zeros((d, LANES), F32)
        wr = wr.at[:, 0:N_GROUPS].set(router_group_w[l]).at[:, 8:8 + N_EXPERTS].set(router_expert_w[l])
        x2, h2p, logits = _merge_out(
            ya.reshape(n, d), yb.reshape(n, d), proj, x2, gate1, norm2_w[l].reshape(1, d), shift2, scale2,
            w_branch_a[l].astype(BF16), w_branch_b[l].astype(BF16), w_out[l].astype(BF16), _split_hi_lo(wr), seq)
        dest, wts, block_e, used, n_rows = _route(logits, router_group_b[l], router_expert_b[l])
        xs = _dispatch(dest, h2p, n_rows)
        ys = _experts(block_e, used, xs, expert_w1[l].astype(BF16), expert_w3[l].astype(BF16),
                      expert_w2[l].astype(BF16))
        x2 = _combine(dest, ys, x2, wts, gate2, final_norm_w.reshape(1, d), seq, final=(l == depth - 1))
    return x2.reshape(bsz, seq, d)
```

```python
import functools
import math

import jax
import jax.numpy as jnp
import numpy as np
from jax import lax
from jax.experimental import pallas as pl
from jax.experimental.pallas import tpu as pltpu

F32 = jnp.float32
BF16 = jnp.bfloat16
HIGHEST = lax.Precision.HIGHEST

D_MODEL = 1024
LANES = 128
ATT_HEADS = 8
ATT_HEAD_DIM = 64
ALIBI_MAX_BIAS = 8.0
GDN_HEADS = 8
GDN_DK = 128
CONV_WIDTH = 4
GDN_CHUNK = 128
N_GROUPS = 4
EXPERTS_PER_GROUP = 8
N_EXPERTS = N_GROUPS * EXPERTS_PER_GROUP
TOP_K = 2
D_EXPERT = D_MODEL // 2
NORM_EPS = 1e-6
NEG_INF = -1e30

C_QA, C_KA, C_VA = 0, 1024, 2048
C_QB, C_KB, C_VB = 3072, 4096, 5120
C_ZB = 6144
C_GATES = 7168
P_MAIN = 9216

ATT_TQ = 1024
ATT_TK = 512
MERGE_SUB = 256
MOE_ROWS = 256
TOK_TILE = 256
DISPATCH_TILE = 1024
DMA_UNROLL = 8
RANK_BLOCK = 256
VMEM_LIMIT = 56 * 1024 * 1024


def _cparams(sem):
    return pltpu.CompilerParams(dimension_semantics=sem, vmem_limit_bytes=VMEM_LIMIT)


def _sigmoid(x):
    return 1.0 / (1.0 + jnp.exp(-x))


def _silu(x):
    return x * _sigmoid(x)


def _dot(a, b):
    return jnp.dot(a, b, preferred_element_type=F32)


def _split_hi_lo(w):
    hi = w.astype(BF16)
    lo = (w - hi.astype(F32)).astype(BF16)
    return jnp.concatenate([hi, lo], axis=1)


def _dot_split(x, w2_ref):
    hi = x.astype(BF16)
    lo = (x - hi.astype(F32)).astype(BF16)
    r = _dot(hi, w2_ref[...])
    return r[:, 0:LANES] + r[:, LANES:2 * LANES] + _dot(lo, w2_ref[:, 0:LANES])


def _dot_nt(a, b):
    return lax.dot_general(a, b, (((1,), (1,)), ((), ())), preferred_element_type=F32)


def _ada_kernel(c_ref, w_ref, b_ref, o_ref):
    cond = _silu(c_ref[...])
    o_ref[0] = jnp.dot(cond, w_ref[0], preferred_element_type=F32, precision=HIGHEST) + b_ref[0]


def _ada_mod(c, ada_w, ada_b):
    depth, d, d6 = ada_w.shape
    bsz = c.shape[0]
    tn = 1536
    return pl.pallas_call(
        _ada_kernel,
        out_shape=jax.ShapeDtypeStruct((depth, bsz, d6), F32),
        grid=(depth, d6 // tn),
        in_specs=[pl.BlockSpec((bsz, d), lambda l, j: (0, 0)),
                  pl.BlockSpec((1, d, tn), lambda l, j: (l, 0, j)),
                  pl.BlockSpec((1, 1, tn), lambda l, j: (l, 0, j))],
        out_specs=pl.BlockSpec((1, bsz, tn), lambda l, j: (l, 0, j)),
        compiler_params=_cparams(("arbitrary", "arbitrary")),
        name="ada_mod",
    )(c, ada_w, ada_b.reshape(depth, 1, d6))


def _inproj_kernel(x_ref, shift_ref, scale_ref, nw_ref, w_ref, ws_ref, o_ref, os_ref, hn_ref):
    @pl.when(pl.program_id(1) == 0)
    def _():
        x = x_ref[...]
        ms = jnp.mean(x * x, axis=-1, keepdims=True)
        hn = x * lax.rsqrt(ms + NORM_EPS) * nw_ref[...]
        hn = hn * (1.0 + scale_ref[0]) + shift_ref[0]
        hn_ref[...] = hn.astype(BF16)
        os_ref[...] = _dot_split(hn, ws_ref)

    o_ref[...] = _dot(hn_ref[...], w_ref[...]).astype(BF16)


def _in_proj(x2, shift, scale, norm_w, w_main, w_small, seq):
    n, d = x2.shape
    tm = min(1024, seq)
    tn = 1024
    per_b = seq // tm
    return pl.pallas_call(
        _inproj_kernel,
        out_shape=(jax.ShapeDtypeStruct((n, P_MAIN), BF16), jax.ShapeDtypeStruct((n, LANES), F32)),
        grid=(n // tm, P_MAIN // tn),
        in_specs=[pl.BlockSpec((tm, d), lambda i, j: (i, 0)),
                  pl.BlockSpec((1, 1, d), lambda i, j: (i // per_b, 0, 0)),
                  pl.BlockSpec((1, 1, d), lambda i, j: (i // per_b, 0, 0)),
                  pl.BlockSpec((1, d), lambda i, j: (0, 0)),
                  pl.BlockSpec((d, tn), lambda i, j: (0, j)),
                  pl.BlockSpec((d, 2 * LANES), lambda i, j: (0, 0))],
        out_specs=(pl.BlockSpec((tm, tn), lambda i, j: (i, j)),
                   pl.BlockSpec((tm, LANES), lambda i, j: (i, 0))),
        scratch_shapes=[pltpu.VMEM((tm, d), BF16)],
        compiler_params=_cparams(("parallel", "arbitrary")),
        name="in_proj",
    )(x2, shift, scale, norm_w, w_main, w_small)


def _bf16_round(x):
    u = int(np.float32(x).view(np.uint32))
    u = (u + 0x7FFF + ((u >> 16) & 1)) & 0xFFFF0000
    return float(np.uint32(u).view(np.float32))


LOG2E = 1.4426950408889634
LOG2E_HI = _bf16_round(LOG2E)
LOG2E_LO = _bf16_round(LOG2E - LOG2E_HI)
N_FEAT = 7


def _feature_lanes(lane, base, values):
    out = jnp.zeros(lane.shape, F32)
    for i, val in enumerate(values):
        out = jnp.where(lane == base + i, val, out)
    return out


def _attn_kernel(slope_ref, q_ref, k_ref, v_ref, lamv_ref, subw_ref, o_ref,
                 ke_ref, ve_ref, qe_ref, m_ref, acc_ref, *, tq, tk, seq, lam_init):
    h = pl.program_id(1)
    qi = pl.program_id(2)
    slope = slope_ref[h]
    hd = ATT_HEAD_DIM
    prep = 512 if seq % 512 == 0 else seq

    @pl.when(qi == 0)
    def _():
        def fill(i, carry):
            r0 = pl.multiple_of(i * prep, prep)
            kf = k_ref[0, pl.ds(r0, prep), :].astype(F32)
            lane = lax.broadcasted_iota(jnp.int32, (prep, LANES), 1)
            j = r0 + lax.broadcasted_iota(jnp.int32, (prep, LANES), 0)
            f_lo = (j & (LANES - 1)).astype(F32) * slope
            f_hi = (j >> (LANES.bit_length() - 1)).astype(F32) * (slope * LANES)
            one = jnp.ones((prep, LANES), F32)
            vals = (f_lo, f_lo, f_hi, f_hi, one, one, one)
            ke_ref[0, pl.ds(r0, prep), :] = jnp.where(lane < hd, kf, _feature_lanes(lane, hd, vals)).astype(BF16)
            ke_ref[1, pl.ds(r0, prep), :] = jnp.where(lane >= hd, kf, _feature_lanes(lane, 0, vals)).astype(BF16)
            ve_ref[pl.ds(r0, prep), 0:LANES] = v_ref[0, pl.ds(r0, prep), :]
            ve_ref[pl.ds(r0, prep), LANES:2 * LANES] = jnp.ones((prep, LANES), BF16)
            return carry

        lax.fori_loop(0, seq // prep, fill, 0)

    q_start = qi * tq
    lane = lax.broadcasted_iota(jnp.int32, (tq, LANES), 1)
    qf = q_ref[0].astype(F32) * (hd ** -0.5 * LOG2E)
    cq = jnp.full((tq, LANES), 1.0, F32) * (slope * (-LOG2E) * q_start.astype(F32))
    c_hi = cq.astype(BF16).astype(F32)
    c_mid = (cq - c_hi).astype(BF16).astype(F32)
    c_lo = cq - c_hi - c_mid
    qvals = (LOG2E_HI, LOG2E_LO, LOG2E_HI, LOG2E_LO, c_hi, c_mid, c_lo)
    qe_ref[0] = jnp.where(lane < hd, qf, _feature_lanes(lane, hd, qvals)).astype(BF16)
    qe_ref[1] = jnp.where(lane >= hd, qf, _feature_lanes(lane, 0, qvals)).astype(BF16)
    m_ref[...] = jnp.full(m_ref.shape, NEG_INF, F32)
    acc_ref[...] = jnp.zeros(acc_ref.shape, F32)

    n_full = q_start // tk

    def blocks(js, masked, r0=0, nr=tq):
        starts = [pl.multiple_of(j * tk, tk) for j in js]
        rows = slice(r0, r0 + nr)
        s_all = [[_dot_nt(qe_ref[mi, rows, :], ke_ref[mi, pl.ds(k0, tk), :]) for mi in range(2)] for k0 in starts]
        if any(masked):
            r = lax.broadcasted_iota(jnp.int32, (nr, tk), 0)
            c = lax.broadcasted_iota(jnp.int32, (nr, tk), 1)
            rel = c - r
        for mi in range(2):
            m = m_ref[mi, rows, :]
            acc = acc_ref[mi, rows, :]
            for bi, k0 in enumerate(starts):
                s = s_all[bi][mi]
                if masked[bi]:
                    s = jnp.where(rel <= (q_start + r0 - k0), s, NEG_INF)
                m_new = jnp.maximum(m, jnp.max(s, axis=1, keepdims=True))
                alpha = jnp.exp2(m - m_new)
                p = jnp.exp2(s - jnp.tile(m_new, (1, tk // LANES))).astype(BF16)
                acc = jnp.tile(alpha, (1, 2)) * acc + _dot(p, ve_ref[pl.ds(k0, tk), :])
                m = m_new
            m_ref[mi, rows, :] = m
            acc_ref[mi, rows, :] = acc

    odd = n_full & 1

    @pl.when(odd == 1)
    def _():
        blocks([0], [False])

    def loop_body(p, carry):
        blocks([odd + 2 * p, odd + 2 * p + 1], [False, False])
        return carry

    lax.fori_loop(0, n_full >> 1, loop_body, 0)
    n_diag = max(1, tq // tk)
    for g in range(n_diag):
        blocks([n_full + d for d in range(g + 1)], [False] * g + [True], g * (tq // n_diag), tq // n_diag)

    lamv = lamv_ref[...]
    lam = (jnp.exp(jnp.sum(lamv[0:1] * lamv[1:2], axis=1, keepdims=True))
           - jnp.exp(jnp.sum(lamv[2:3] * lamv[3:4], axis=1, keepdims=True)) + lam_init)
    a1 = acc_ref[0]
    a2 = acc_ref[1]
    o = a1[:, 0:LANES] / a1[:, LANES:2 * LANES] - lam * (a2[:, 0:LANES] / a2[:, LANES:2 * LANES])
    y = o * lax.rsqrt(jnp.mean(o * o, axis=-1, keepdims=True) + NORM_EPS)
    o_ref[0] = (y * subw_ref[...] * (1.0 - lam_init)).astype(BF16)


def _diff_attention(proj3, lamv, subln_w, lam_init):
    bsz, seq, _ = proj3.shape
    tq = min(ATT_TQ, seq)
    tk = min(ATT_TK, seq)
    slopes = jnp.exp2(-ALIBI_MAX_BIAS * jnp.arange(1, ATT_HEADS + 1, dtype=F32) / ATT_HEADS)
    kern = functools.partial(_attn_kernel, tq=tq, tk=tk, seq=seq, lam_init=lam_init)
    nq = C_QA // LANES
    nk = C_KA // LANES
    nv = C_VA // LANES
    return pl.pallas_call(
        kern,
        out_shape=jax.ShapeDtypeStruct((bsz, seq, ATT_HEADS * LANES), BF16),
        grid_spec=pltpu.PrefetchScalarGridSpec(
            num_scalar_prefetch=1,
            grid=(bsz, ATT_HEADS, seq // tq),
            in_specs=[pl.BlockSpec((1, tq, LANES), lambda b, h, i, s: (b, i, nq + h)),
                      pl.BlockSpec((1, seq, LANES), lambda b, h, i, s: (b, 0, nk + h)),
                      pl.BlockSpec((1, seq, LANES), lambda b, h, i, s: (b, 0, nv + h)),
                      pl.BlockSpec((4, LANES), lambda b, h, i, s: (0, 0)),
                      pl.BlockSpec((1, LANES), lambda b, h, i, s: (0, 0))],
            out_specs=pl.BlockSpec((1, tq, LANES), lambda b, h, i, s: (b, i, h)),
            scratch_shapes=[pltpu.VMEM((2, seq, LANES), BF16),
                            pltpu.VMEM((seq, 2 * LANES), BF16),
                            pltpu.VMEM((2, tq, LANES), BF16),
                            pltpu.VMEM((2, tq, LANES), F32),
                            pltpu.VMEM((2, tq, 2 * LANES), F32)]),
        compiler_params=_cparams(("parallel", "arbitrary", "arbitrary")),
        name="diff_attn",
    )(slopes, proj3, proj3, proj3, lamv, subln_w)


def _gdn_kernel(q_ref, k_ref, v_ref, z_ref, small_ref, cwq_ref, cwk_ref, cwv_ref,
                alog_ref, dtb_ref, nw_ref, o_ref, xbuf_ref, pre_ref, state_ref, *, heads, chunk):
    s_idx = pl.program_id(1)
    width = heads * LANES
    pad = 8

    @pl.when(s_idx == 0)
    def _():
        xbuf_ref[:, 0:pad, :] = jnp.zeros((3, pad, width), F32)
        pre_ref[...] = jnp.zeros(pre_ref.shape, F32)
        state_ref[...] = jnp.zeros(state_ref.shape, F32)

    def conv_silu(idx, raw_ref, cw_ref, sl, anchor):
        xbuf_ref[idx, pad:pad + chunk, sl] = raw_ref[0, :, sl].astype(F32)
        cw = cw_ref[:, sl] + anchor
        acc = xbuf_ref[idx, pad:pad + chunk, sl] * cw[CONV_WIDTH - 1:CONV_WIDTH]
        for back in range(1, CONV_WIDTH):
            tap = CONV_WIDTH - 1 - back
            acc = acc + xbuf_ref[idx, pad - back:pad - back + chunk, sl] * cw[tap:tap + 1]
        xbuf_ref[idx, 0:pad, sl] = xbuf_ref[idx, chunk:chunk + pad, sl]
        return _silu(acc)

    small = small_ref[0]
    beta_all = _sigmoid(small)
    sp_in = small + dtb_ref[...]
    softplus = jnp.maximum(sp_in, 0.0) + jnp.log(1.0 + jnp.exp(-jnp.abs(sp_in)))
    g_all = -jnp.exp(alog_ref[...]) * softplus
    row = lax.broadcasted_iota(jnp.int32, (chunk, chunk), 0)
    col = lax.broadcasted_iota(jnp.int32, (chunk, chunk), 1)
    incl = col <= row
    strict = col < row
    tri = jnp.where(incl, 1.0, 0.0).astype(F32)
    gc_all = jnp.dot(tri, g_all, preferred_element_type=F32, precision=HIGHEST)
    eye = jnp.where(row == col, 1.0, 0.0).astype(F32)

    level_masks = []
    for k in range(chunk.bit_length() - 1):
        same = (row >> (k + 1)) == (col >> (k + 1))
        level_masks.append(same & (((row >> k) & 1) == 1) & (((col >> k) & 1) == 0))

    nw = nw_ref[...]
    hs = range(heads)
    sls = [slice(hh * LANES, (hh + 1) * LANES) for hh in hs]

    def prepare(hh, anchor):
        sl = sls[hh]
        qh = conv_silu(0, q_ref, cwq_ref, sl, anchor)
        kh = conv_silu(1, k_ref, cwk_ref, sl, anchor)
        pre_ref[0, :, sl] = qh * lax.rsqrt(jnp.sum(qh * qh, axis=-1, keepdims=True) + NORM_EPS) * (GDN_DK ** -0.5)
        pre_ref[1, :, sl] = kh * lax.rsqrt(jnp.sum(kh * kh, axis=-1, keepdims=True) + NORM_EPS)
        pre_ref[2, :, sl] = conv_silu(2, v_ref, cwv_ref, sl, anchor)

    todo = list(hs)

    def prepare_some(count, after):
        anchor = after[0:CONV_WIDTH, 0:LANES] * 0.0
        for _ in range(min(count, len(todo))):
            prepare(todo.pop(0), anchor)

    qn = [pre_ref[0, :, sls[hh]] for hh in hs]
    kn = [pre_ref[1, :, sls[hh]] for hh in hs]
    vc = [pre_ref[2, :, sls[hh]] for hh in hs]
    kb, gcs, egs, decay = [], [], [], []
    for hh in hs:
        kb.append(kn[hh] * beta_all[:, hh:hh + 1])
        gc = gc_all[:, heads + hh:heads + hh + 1]
        gcs.append(gc)
        egs.append(jnp.exp(gc))
        g_col = jnp.broadcast_to(gc, (chunk, chunk))
        decay.append(jnp.where(incl, jnp.exp(jnp.where(incl, g_col - g_col.T, 0.0)), 0.0))
    kn16 = [kn[hh].astype(BF16) for hh in hs]
    kk = [_dot_nt(kb[hh].astype(BF16), kn16[hh]) for hh in hs]
    qk = [_dot_nt(qn[hh].astype(BF16), kn16[hh]) for hh in hs]
    low = [jnp.where(strict, kk[hh] * decay[hh], 0.0) for hh in hs]
    intra = [(qk[hh] * decay[hh]).astype(BF16) for hh in hs]
    inv = [eye - jnp.where(level_masks[0], low[hh], 0.0) for hh in hs]
    for m in level_masks[1:]:
        inv16 = [inv[hh].astype(BF16) for hh in hs]
        t = [_dot(jnp.where(m, low[hh], 0.0).astype(BF16), inv16[hh]).astype(BF16) for hh in hs]
        inv = [inv[hh] - _dot(inv16[hh], t[hh]) for hh in hs]
        prepare_some(1, inv[0])
    rhs = [jnp.concatenate([vc[hh] * beta_all[:, hh:hh + 1], kb[hh] * egs[hh]], axis=1).astype(BF16)
           for hh in hs]
    uw = [_dot(inv[hh].astype(BF16), rhs[hh]) for hh in hs]
    prepare_some(heads, uw[0])
    state = [state_ref[hh] for hh in hs]
    state16 = [state[hh].astype(BF16) for hh in hs]
    v_new = [(uw[hh][:, 0:LANES] - _dot(uw[hh][:, LANES:2 * LANES].astype(BF16), state16[hh])).astype(BF16)
             for hh in hs]
    for hh in hs:
        g_last = gcs[hh][chunk - 1:chunk, :]
        kdec = kn[hh] * jnp.exp(g_last - gcs[hh])
        state_ref[hh] = state[hh] * jnp.exp(g_last) + _dot(kdec.T.astype(BF16), v_new[hh])
    for hh in hs:
        o = _dot((qn[hh] * egs[hh]).astype(BF16), state16[hh]) + _dot(intra[hh], v_new[hh])
        y = o * lax.rsqrt(jnp.mean(o * o, axis=-1, keepdims=True) + NORM_EPS) * nw
        zf = z_ref[0, :, sls[hh]].astype(F32)
        o_ref[0, :, sls[hh]] = (y * _silu(zf)).astype(BF16)


def _gated_deltanet(proj3, small3, conv_w, a_log, dt_bias, norm_w):
    bsz, seq, _ = proj3.shape
    heads = GDN_HEADS
    width = heads * LANES
    chunk = min(GDN_CHUNK, seq)
    zeros = jnp.zeros((LANES - 2 * heads,), F32)
    alog_row = jnp.concatenate([jnp.zeros((heads,), F32), a_log, zeros]).reshape(1, LANES)
    dtb_row = jnp.concatenate([jnp.zeros((heads,), F32), dt_bias, zeros]).reshape(1, LANES)
    kern = functools.partial(_gdn_kernel, heads=heads, chunk=chunk)
    n_chunks = seq // chunk
    nxt = lambda s: jnp.minimum(s, n_chunks - 1)
    cur = lambda s: jnp.maximum(s - 1, 0)
    raw = lambda off: pl.BlockSpec((1, chunk, width), lambda b, s: (b, nxt(s), off // width))
    cw = lambda j: pl.BlockSpec((CONV_WIDTH, width), lambda b, s: (0, j))
    row = pl.BlockSpec((1, LANES), lambda b, s: (0, 0))
    return pl.pallas_call(
        kern,
        out_shape=jax.ShapeDtypeStruct((bsz, seq, width), BF16),
        grid=(bsz, n_chunks + 1),
        in_specs=[raw(C_QB), raw(C_KB), raw(C_VB),
                  pl.BlockSpec((1, chunk, width), lambda b, s: (b, cur(s), C_ZB // width)),
                  pl.BlockSpec((1, chunk, LANES), lambda b, s: (b, cur(s), 0)),
                  cw(0), cw(1), cw(2), row, row, row],
        out_specs=pl.BlockSpec((1, chunk, width), lambda b, s: (b, cur(s), 0)),
        scratch_shapes=[pltpu.VMEM((3, chunk + 8, width), F32),
                        pltpu.VMEM((3, chunk, width), F32),
                        pltpu.VMEM((heads, GDN_DK, LANES), F32)],
        compiler_params=_cparams(("parallel", "arbitrary")),
        name="gated_deltanet",
    )(proj3, proj3, proj3, proj3, small3, conv_w, conv_w, conv_w, alog_row, dtb_row, norm_w)


def _pack_bf16_pairs(x):
    k = x.shape[1] // 2
    r = pltpu.bitcast(x.astype(BF16).astype(F32), jnp.uint32)
    return (r[:, k:] & jnp.uint32(0xFFFF0000)) | (r[:, :k] >> 16)


def _unpack_bf16_pairs(p):
    lo = pltpu.bitcast(p << 16, F32)
    hi = pltpu.bitcast(p & jnp.uint32(0xFFFF0000), F32)
    return jnp.concatenate([lo, hi], axis=1).astype(BF16)


def _merge_kernel(ya_ref, yb_ref, ga_ref, gb_ref, x_ref, gate1_ref, nw_ref, shift_ref, scale_ref,
                  wa_ref, wb_ref, wo_ref, wr_ref, xo_ref, h2_ref, lg_ref):
    tm = x_ref.shape[0]
    sub = min(MERGE_SUB, tm)
    rows = [slice(i * sub, (i + 1) * sub) for i in range(tm // sub)]
    a = [_dot(ya_ref[r, :], wa_ref[...]) for r in rows]
    b = [_dot(yb_ref[r, :], wb_ref[...]) for r in rows]
    mixed = [(_sigmoid(ga_ref[r, :].astype(F32)) * a[i] + _sigmoid(gb_ref[r, :].astype(F32)) * b[i]).astype(BF16)
             for i, r in enumerate(rows)]
    upd = [_dot(m, wo_ref[...]) for m in mixed]
    for i, r in enumerate(rows):
        xn = x_ref[r, :] + gate1_ref[0] * upd[i]
        xo_ref[r, :] = xn
        ms = jnp.mean(xn * xn, axis=-1, keepdims=True)
        h2 = xn * lax.rsqrt(ms + NORM_EPS) * nw_ref[...]
        h2 = h2 * (1.0 + scale_ref[0]) + shift_ref[0]
        h2_ref[r, :] = _pack_bf16_pairs(h2)
        lg_ref[r, :] = _dot_split(h2, wr_ref)


def _merge_out(ya, yb, proj, x2, gate1, norm_w, shift, scale, wa, wb, wo, wr, seq):
    n, d = x2.shape
    tm = min(512, seq)
    per_b = seq // tm
    g0 = C_GATES // d
    tok = lambda j: pl.BlockSpec((tm, d), lambda i: (i, j))
    per_batch = pl.BlockSpec((1, 1, d), lambda i: (i // per_b, 0, 0))
    full = lambda r, c: pl.BlockSpec((r, c), lambda i: (0, 0))
    return pl.pallas_call(
        _merge_kernel,
        out_shape=(jax.ShapeDtypeStruct((n, d), F32),
                   jax.ShapeDtypeStruct((n, d // 2), jnp.uint32),
                   jax.ShapeDtypeStruct((n, LANES), F32)),
        grid=(n // tm,),
        in_specs=[tok(0), tok(0), tok(g0), tok(g0 + 1), tok(0), per_batch, full(1, d), per_batch, per_batch,
                  full(d, d), full(d, d), full(d, d), full(d, 2 * LANES)],
        out_specs=(tok(0), pl.BlockSpec((tm, d // 2), lambda i: (i, 0)),
                   pl.BlockSpec((tm, LANES), lambda i: (i, 0))),
        compiler_params=_cparams(("parallel",)),
        name="merge_out",
    )(ya, yb, proj, proj, x2, gate1, norm_w, shift, scale, wa, wb, wo, wr)


def _dispatch_kernel(dest_ref, h2_ref, xs_in_ref, xs_ref, sem, *, tile):
    del xs_in_ref

    def row_copy(t, k):
        return pltpu.make_async_copy(h2_ref.at[pl.ds(t, 1), :],
                                     xs_ref.at[pl.ds(dest_ref[t * TOP_K + k], 1), :], sem)

    def start(t, carry):
        for k in range(TOP_K):
            row_copy(t, k).start()
        return carry

    def wait(t, carry):
        for k in range(TOP_K):
            row_copy(t, k).wait()
        return carry

    lax.fori_loop(0, tile, start, 0, unroll=DMA_UNROLL)
    lax.fori_loop(0, tile, wait, 0, unroll=DMA_UNROLL)


def _dispatch(dest_flat, h2p, n_rows):
    n, half = h2p.shape
    tile = min(DISPATCH_TILE, n)
    xs0 = jnp.zeros((n_rows, half), jnp.uint32)
    kern = functools.partial(_dispatch_kernel, tile=tile)
    return pl.pallas_call(
        kern,
        out_shape=jax.ShapeDtypeStruct((n_rows, half), jnp.uint32),
        grid=(n // tile,),
        in_specs=[pl.BlockSpec((tile * TOP_K,), lambda i: (i,), memory_space=pltpu.SMEM),
                  pl.BlockSpec((tile, half), lambda i: (i, 0)),
                  pl.BlockSpec(memory_space=pl.ANY)],
        out_specs=pl.BlockSpec(memory_space=pl.ANY),
        scratch_shapes=[pltpu.SemaphoreType.DMA],
        input_output_aliases={2: 0},
        compiler_params=_cparams(("arbitrary",)),
        name="moe_dispatch",
    )(dest_flat, h2p, xs0)


def _expert_kernel(be_ref, used_ref, x_ref, w1_ref, w3_ref, w2_ref, y_ref, w1b_ref, w3b_ref, w2b_ref):
    i = pl.program_id(0)
    live = i < used_ref[0]
    new_expert = jnp.logical_or(i == 0, be_ref[i] != be_ref[jnp.maximum(i - 1, 0)])

    @pl.when(new_expert)
    def _():
        w1b_ref[...] = w1_ref[0, 0].astype(BF16)
        w3b_ref[...] = w3_ref[0, 0].astype(BF16)
        w2b_ref[...] = w2_ref[0, 0].astype(BF16)

    @pl.when(live)
    def _():
        x = _unpack_bf16_pairs(x_ref[...])
        a = _dot(x, w1b_ref[...])
        g = _dot(x, w3b_ref[...])
        y_ref[...] = _dot((_silu(a) * g).astype(BF16), w2b_ref[...])

    @pl.when(jnp.logical_not(live))
    def _():
        y_ref[...] = jnp.zeros(y_ref.shape, F32)


def _experts(block_e, used, xs, w1, w3, w2, layer):
    n_rows, half = xs.shape
    d = 2 * half
    n_blocks = n_rows // MOE_ROWS
    live = lambda i, be, used: jnp.minimum(i, used[0] - 1)
    wspec = lambda r, c: pl.BlockSpec((1, 1, r, c), lambda i, be, used: (layer, be[i], 0, 0))
    return pl.pallas_call(
        _expert_kernel,
        out_shape=jax.ShapeDtypeStruct((n_rows, d), F32),
        grid_spec=pltpu.PrefetchScalarGridSpec(
            num_scalar_prefetch=2,
            grid=(n_blocks,),
            in_specs=[pl.BlockSpec((MOE_ROWS, half), lambda i, be, used: (live(i, be, used), 0)),
                      wspec(d, D_EXPERT), wspec(d, D_EXPERT), wspec(D_EXPERT, d)],
            out_specs=pl.BlockSpec((MOE_ROWS, d), lambda i, be, used: (i, 0)),
            scratch_shapes=[pltpu.VMEM((d, D_EXPERT), BF16), pltpu.VMEM((d, D_EXPERT), BF16),
                            pltpu.VMEM((D_EXPERT, d), BF16)]),
        compiler_params=_cparams(("arbitrary",)),
        name="moe_experts",
    )(block_e, used, xs, w1, w3, w2)


def _combine_kernel(dcur_ref, dnxt_ref, ys_ref, x_ref, wt_ref, gate2_ref, fw_ref, o_ref, buf_ref, sems,
                    *, tile, final):
    i = pl.program_id(0)
    slot = i & 1

    def row_copy(d_ref, t, k, s):
        return pltpu.make_async_copy(ys_ref.at[pl.ds(d_ref[t * TOP_K + k], 1), :],
                                     buf_ref.at[s, k, pl.ds(t, 1), :], sems.at[s])

    def gather(d_ref, s):
        def start(t, carry):
            for k in range(TOP_K):
                row_copy(d_ref, t, k, s).start()
            return carry

        lax.fori_loop(0, tile, start, 0, unroll=DMA_UNROLL)

    @pl.when(i == 0)
    def _():
        gather(dcur_ref, 0)

    @pl.when(i < pl.num_programs(0) - 1)
    def _():
        gather(dnxt_ref, 1 - slot)

    def wait(t, carry):
        for k in range(TOP_K):
            row_copy(dcur_ref, t, k, slot).wait()
        return carry

    lax.fori_loop(0, tile, wait, 0, unroll=DMA_UNROLL)
    wt = wt_ref[...]
    y = wt[:, 0:1] * buf_ref[slot, 0] + wt[:, 1:2] * buf_ref[slot, 1]
    xn = x_ref[...] + gate2_ref[0] * y
    if final:
        xn = xn * lax.rsqrt(jnp.mean(xn * xn, axis=-1, keepdims=True) + NORM_EPS) * fw_ref[...]
    o_ref[...] = xn


def _combine(dest_flat, ys, x2, wts, gate2, final_w, seq, final):
    n, d = x2.shape
    tile = min(TOK_TILE, seq)
    per_b = seq // tile
    steps = n // tile
    kern = functools.partial(_combine_kernel, tile=tile, final=final)
    return pl.pallas_call(
        kern,
        out_shape=jax.ShapeDtypeStruct((n, d), F32),
        grid=(steps,),
        in_specs=[pl.BlockSpec((tile * TOP_K,), lambda i: (i,), memory_space=pltpu.SMEM),
                  pl.BlockSpec((tile * TOP_K,), lambda i: (jnp.minimum(i + 1, steps - 1),),
                               memory_space=pltpu.SMEM),
                  pl.BlockSpec(memory_space=pl.ANY),
                  pl.BlockSpec((tile, d), lambda i: (i, 0)),
                  pl.BlockSpec((tile, TOP_K), lambda i: (i, 0)),
                  pl.BlockSpec((1, 1, d), lambda i: (i // per_b, 0, 0)),
                  pl.BlockSpec((1, d), lambda i: (0, 0))],
        out_specs=pl.BlockSpec((tile, d), lambda i: (i, 0)),
        scratch_shapes=[pltpu.VMEM((2, TOP_K, tile, d), F32), pltpu.SemaphoreType.DMA((2,))],
        compiler_params=_cparams(("arbitrary",)),
        name="moe_combine",
    )(dest_flat, dest_flat, ys, x2, wts, gate2, final_w)


def _route(logits, b_rg, b_re):
    n = logits.shape[0]
    gl = logits[:, 0:N_GROUPS] + b_rg
    el = logits[:, 8:8 + N_EXPERTS] + b_re
    gprob = jax.nn.softmax(gl, axis=-1)
    gsel = jnp.argmax(gprob, axis=-1)
    gweight = jnp.take_along_axis(gprob, gsel[:, None], axis=-1)
    el = el.reshape(n, N_GROUPS, EXPERTS_PER_GROUP)
    el = jnp.take_along_axis(el, gsel[:, None, None], axis=1)[:, 0]
    top_p, top_i = lax.top_k(jax.nn.softmax(el, axis=-1), TOP_K)
    weights = gweight * top_p / jnp.sum(top_p, axis=-1, keepdims=True)
    expert_ids = (gsel[:, None] * EXPERTS_PER_GROUP + top_i).astype(jnp.int32)
    flat_e = expert_ids.reshape(-1)
    onehot = (flat_e[:, None] == jnp.arange(N_EXPERTS, dtype=jnp.int32)[None, :]).astype(F32)
    rb = min(RANK_BLOCK, n * TOP_K)
    oh = onehot.reshape(-1, rb, N_EXPERTS)
    tri = jnp.tril(jnp.ones((rb, rb), F32))
    within = jnp.einsum('ij,bjk->bik', tri, oh)
    totals = within[:, -1, :]
    base = jnp.cumsum(totals, axis=0) - totals
    rank = (jnp.sum((within + base[:, None, :]) * oh, axis=-1).reshape(-1) - 1.0).astype(jnp.int32)
    counts = jnp.sum(totals, axis=0).astype(jnp.int32)
    padded = (counts + MOE_ROWS - 1) // MOE_ROWS * MOE_ROWS
    pad_end = jnp.cumsum(padded)
    pad_start = pad_end - padded
    dest = (pad_start[flat_e] + rank).astype(jnp.int32)
    n_blocks = (n * TOP_K) // MOE_ROWS + N_EXPERTS
    used = jnp.maximum(pad_end[-1] // MOE_ROWS, 1).astype(jnp.int32)
    blk = jnp.minimum(jnp.arange(n_blocks, dtype=jnp.int32), used - 1)
    block_e = jnp.minimum(jnp.searchsorted(pad_end, blk * MOE_ROWS, side='right'),
                          N_EXPERTS - 1).astype(jnp.int32)
    return dest, weights.astype(F32), block_e, used.reshape(1), n_blocks * MOE_ROWS


def kernel(x, c, ada_w, ada_b, norm1_w, w_in, conv_w, lambda_q1, lambda_k1, lambda_q2, lambda_k2,
           subln_w, a_log, dt_bias, gdn_norm_w, w_branch_a, w_branch_b, w_out, norm2_w,
           router_group_w, router_group_b, router_expert_w, router_expert_b,
           expert_w1, expert_w3, expert_w2, final_norm_w):
    bsz, seq, d = x.shape
    depth = ada_w.shape[0]
    n = bsz * seq
    mod = _ada_mod(c, ada_w, ada_b)
    x2 = x.reshape(n, d)
    n_beta = 7168
    for l in range(depth):
        lam_init = 0.8 - 0.6 * math.exp(-0.3 * l)
        m = mod[l].reshape(bsz, 1, 6, d)
        shift1, scale1, gate1, shift2, scale2, gate2 = (m[:, :, i] for i in range(6))
        w = w_in[l]
        w_main = jnp.concatenate([w[:, :n_beta], w[:, n_beta + 2 * GDN_HEADS:]], axis=1).astype(BF16)
        w_small = _split_hi_lo(jnp.pad(w[:, n_beta:n_beta + 2 * GDN_HEADS], ((0, 0), (0, LANES - 2 * GDN_HEADS))))
        proj, small = _in_proj(x2, shift1, scale1, norm1_w[l].reshape(1, d), w_main, w_small, seq)
        proj3 = proj.reshape(bsz, seq, P_MAIN)
        lamv = jnp.pad(jnp.stack([lambda_q1[l], lambda_k1[l], lambda_q2[l], lambda_k2[l]]),
                       ((0, 0), (0, LANES - ATT_HEAD_DIM)))
        ya = _diff_attention(proj3, lamv, subln_w[l].reshape(1, LANES), lam_init)
        yb = _gated_deltanet(proj3, small.reshape(bsz, seq, LANES), conv_w[l], a_log[l], dt_bias[l],
                             gdn_norm_w[l].reshape(1, LANES))
        wr = jnp.zeros((d, LANES), F32)
        wr = wr.at[:, 0:N_GROUPS].set(router_group_w[l]).at[:, 8:8 + N_EXPERTS].set(router_expert_w[l])
        x2, h2p, logits = _merge_out(
            ya.reshape(n, d), yb.reshape(n, d), proj, x2, gate1, norm2_w[l].reshape(1, d), shift2, scale2,
            w_branch_a[l].astype(BF16), w_branch_b[l].astype(BF16), w_out[l].astype(BF16), _split_hi_lo(wr), seq)
        dest, wts, block_e, used, n_rows = _route(logits, router_group_b[l], router_expert_b[l])
        xs = _dispatch(dest, h2p, n_rows)
        ys = _experts(block_e, used, xs, expert_w1, expert_w3, expert_w2, l)
        x2 = _combine(dest, ys, x2, wts, gate2, final_norm_w.reshape(1, d), seq, final=(l == depth - 1))
    return x2.reshape(bsz, seq, d)
```

```python
import functools
import math

import jax
import jax.numpy as jnp
import numpy as np
from jax import lax
from jax.experimental import pallas as pl
from jax.experimental.pallas import tpu as pltpu

F32 = jnp.float32
BF16 = jnp.bfloat16
HIGHEST = lax.Precision.HIGHEST

D_MODEL = 1024
LANES = 128
ATT_HEADS = 8
ATT_HEAD_DIM = 64
ALIBI_MAX_BIAS = 8.0
GDN_HEADS = 8
GDN_DK = 128
CONV_WIDTH = 4
GDN_CHUNK = 128
N_GROUPS = 4
EXPERTS_PER_GROUP = 8
N_EXPERTS = N_GROUPS * EXPERTS_PER_GROUP
TOP_K = 2
D_EXPERT = D_MODEL // 2
NORM_EPS = 1e-6
NEG_INF = -1e30

C_QA, C_KA, C_VA = 0, 1024, 2048
C_QB, C_KB, C_VB = 3072, 4096, 5120
C_ZB = 6144
C_GATES = 7168
P_MAIN = 9216

ATT_TQ = 1024
ATT_TK = 512
MERGE_SUB = 256
MOE_ROWS = 256
TOK_TILE = 256
DISPATCH_TILE = 512
DMA_UNROLL = 8
ROUTE_TILE = 2048
VMEM_LIMIT = 56 * 1024 * 1024


def _cparams(sem):
    return pltpu.CompilerParams(dimension_semantics=sem, vmem_limit_bytes=VMEM_LIMIT)


def _sigmoid(x):
    return 1.0 / (1.0 + jnp.exp(-x))


def _silu(x):
    return x * _sigmoid(x)


def _dot(a, b):
    return jnp.dot(a, b, preferred_element_type=F32)


def _split_hi_lo(w):
    hi = w.astype(BF16)
    lo = (w - hi.astype(F32)).astype(BF16)
    return jnp.concatenate([hi, lo], axis=1)


def _dot_split(x, w2_ref):
    hi = x.astype(BF16)
    lo = (x - hi.astype(F32)).astype(BF16)
    r = _dot(hi, w2_ref[...])
    return r[:, 0:LANES] + r[:, LANES:2 * LANES] + _dot(lo, w2_ref[:, 0:LANES])


def _dot_nt(a, b):
    return lax.dot_general(a, b, (((1,), (1,)), ((), ())), preferred_element_type=F32)


def _ada_kernel(c_ref, w_ref, b_ref, o_ref):
    cond = _silu(c_ref[...])
    o_ref[0] = jnp.dot(cond, w_ref[0], preferred_element_type=F32, precision=HIGHEST) + b_ref[0]


def _ada_mod(c, ada_w, ada_b):
    depth, d, d6 = ada_w.shape
    bsz = c.shape[0]
    tn = 1536
    return pl.pallas_call(
        _ada_kernel,
        out_shape=jax.ShapeDtypeStruct((depth, bsz, d6), F32),
        grid=(depth, d6 // tn),
        in_specs=[pl.BlockSpec((bsz, d), lambda l, j: (0, 0)),
                  pl.BlockSpec((1, d, tn), lambda l, j: (l, 0, j)),
                  pl.BlockSpec((1, 1, tn), lambda l, j: (l, 0, j))],
        out_specs=pl.BlockSpec((1, bsz, tn), lambda l, j: (l, 0, j)),
        compiler_params=_cparams(("arbitrary", "arbitrary")),
        name="ada_mod",
    )(c, ada_w, ada_b.reshape(depth, 1, d6))


def _inproj_kernel(x_ref, shift_ref, scale_ref, nw_ref, w_ref, ws_ref, o_ref, os_ref, hn_ref):
    @pl.when(pl.program_id(1) == 0)
    def _():
        x = x_ref[...]
        ms = jnp.mean(x * x, axis=-1, keepdims=True)
        hn = x * lax.rsqrt(ms + NORM_EPS) * nw_ref[...]
        hn = hn * (1.0 + scale_ref[0]) + shift_ref[0]
        hn_ref[...] = hn.astype(BF16)
        os_ref[...] = _dot_split(hn, ws_ref)

    o_ref[...] = _dot(hn_ref[...], w_ref[...]).astype(BF16)


def _in_proj(x2, shift, scale, norm_w, w_main, w_small, seq):
    n, d = x2.shape
    tm = min(1024, seq)
    tn = 3072
    per_b = seq // tm
    return pl.pallas_call(
        _inproj_kernel,
        out_shape=(jax.ShapeDtypeStruct((n, P_MAIN), BF16), jax.ShapeDtypeStruct((n, LANES), F32)),
        grid=(n // tm, P_MAIN // tn),
        in_specs=[pl.BlockSpec((tm, d), lambda i, j: (i, 0)),
                  pl.BlockSpec((1, 1, d), lambda i, j: (i // per_b, 0, 0)),
                  pl.BlockSpec((1, 1, d), lambda i, j: (i // per_b, 0, 0)),
                  pl.BlockSpec((1, d), lambda i, j: (0, 0)),
                  pl.BlockSpec((d, tn), lambda i, j: (0, j)),
                  pl.BlockSpec((d, 2 * LANES), lambda i, j: (0, 0))],
        out_specs=(pl.BlockSpec((tm, tn), lambda i, j: (i, j)),
                   pl.BlockSpec((tm, LANES), lambda i, j: (i, 0))),
        scratch_shapes=[pltpu.VMEM((tm, d), BF16)],
        compiler_params=_cparams(("parallel", "arbitrary")),
        name="in_proj",
    )(x2, shift, scale, norm_w, w_main, w_small)


def _bf16_round(x):
    u = int(np.float32(x).view(np.uint32))
    u = (u + 0x7FFF + ((u >> 16) & 1)) & 0xFFFF0000
    return float(np.uint32(u).view(np.float32))


LOG2E = 1.4426950408889634
LOG2E_HI = _bf16_round(LOG2E)
LOG2E_LO = _bf16_round(LOG2E - LOG2E_HI)
N_FEAT = 7


def _feature_lanes(lane, base, values):
    out = jnp.zeros(lane.shape, F32)
    for i, val in enumerate(values):
        out = jnp.where(lane == base + i, val, out)
    return out


def _attn_kernel(slope_ref, q_ref, k_ref, v_ref, lamv_ref, subw_ref, o_ref,
                 ke_ref, ve_ref, qe_ref, m_ref, acc_ref, *, tq, tk, seq, lam_init):
    h = pl.program_id(1)
    qi = pl.program_id(2)
    slope = slope_ref[h]
    hd = ATT_HEAD_DIM
    prep = 512 if seq % 512 == 0 else seq

    @pl.when(qi == 0)
    def _():
        def fill(i, carry):
            r0 = pl.multiple_of(i * prep, prep)
            kf = k_ref[0, pl.ds(r0, prep), :].astype(F32)
            lane = lax.broadcasted_iota(jnp.int32, (prep, LANES), 1)
            j = r0 + lax.broadcasted_iota(jnp.int32, (prep, LANES), 0)
            f_lo = (j & (LANES - 1)).astype(F32) * slope
            f_hi = (j >> (LANES.bit_length() - 1)).astype(F32) * (slope * LANES)
            one = jnp.ones((prep, LANES), F32)
            vals = (f_lo, f_lo, f_hi, f_hi, one, one, one)
            ke_ref[0, pl.ds(r0, prep), :] = jnp.where(lane < hd, kf, _feature_lanes(lane, hd, vals)).astype(BF16)
            ke_ref[1, pl.ds(r0, prep), :] = jnp.where(lane >= hd, kf, _feature_lanes(lane, 0, vals)).astype(BF16)
            ve_ref[pl.ds(r0, prep), 0:LANES] = v_ref[0, pl.ds(r0, prep), :]
            ve_ref[pl.ds(r0, prep), LANES:2 * LANES] = jnp.ones((prep, LANES), BF16)
            return carry

        lax.fori_loop(0, seq // prep, fill, 0)

    q_start = qi * tq
    lane = lax.broadcasted_iota(jnp.int32, (tq, LANES), 1)
    qf = q_ref[0].astype(F32) * (hd ** -0.5 * LOG2E)
    cq = jnp.full((tq, LANES), 1.0, F32) * (slope * (-LOG2E) * q_start.astype(F32))
    c_hi = cq.astype(BF16).astype(F32)
    c_mid = (cq - c_hi).astype(BF16).astype(F32)
    c_lo = cq - c_hi - c_mid
    qvals = (LOG2E_HI, LOG2E_LO, LOG2E_HI, LOG2E_LO, c_hi, c_mid, c_lo)
    qe_ref[0] = jnp.where(lane < hd, qf, _feature_lanes(lane, hd, qvals)).astype(BF16)
    qe_ref[1] = jnp.where(lane >= hd, qf, _feature_lanes(lane, 0, qvals)).astype(BF16)
    m_ref[...] = jnp.full(m_ref.shape, NEG_INF, F32)
    acc_ref[...] = jnp.zeros(acc_ref.shape, F32)

    n_full = q_start // tk

    def blocks(js, masked, r0=0, nr=tq):
        starts = [pl.multiple_of(j * tk, tk) for j in js]
        rows = slice(r0, r0 + nr)
        s_all = [[_dot_nt(qe_ref[mi, rows, :], ke_ref[mi, pl.ds(k0, tk), :]) for mi in range(2)] for k0 in starts]
        if any(masked):
            r = lax.broadcasted_iota(jnp.int32, (nr, tk), 0)
            c = lax.broadcasted_iota(jnp.int32, (nr, tk), 1)
            rel = c - r
        for mi in range(2):
            m = m_ref[mi, rows, :]
            acc = acc_ref[mi, rows, :]
            for bi, k0 in enumerate(starts):
                s = s_all[bi][mi]
                if masked[bi]:
                    s = jnp.where(rel <= (q_start + r0 - k0), s, NEG_INF)
                m_new = jnp.maximum(m, jnp.max(s, axis=1, keepdims=True))
                alpha = jnp.exp2(m - m_new)
                p = jnp.exp2(s - jnp.tile(m_new, (1, tk // LANES))).astype(BF16)
                acc = jnp.tile(alpha, (1, 2)) * acc + _dot(p, ve_ref[pl.ds(k0, tk), :])
                m = m_new
            m_ref[mi, rows, :] = m
            acc_ref[mi, rows, :] = acc

    odd = n_full & 1

    @pl.when(odd == 1)
    def _():
        blocks([0], [False])

    def loop_body(p, carry):
        blocks([odd + 2 * p, odd + 2 * p + 1], [False, False])
        return carry

    lax.fori_loop(0, n_full >> 1, loop_body, 0)
    n_diag = max(1, tq // tk)
    for g in range(n_diag):
        blocks([n_full + d for d in range(g + 1)], [False] * g + [True], g * (tq // n_diag), tq // n_diag)

    lamv = lamv_ref[...]
    lam = (jnp.exp(jnp.sum(lamv[0:1] * lamv[1:2], axis=1, keepdims=True))
           - jnp.exp(jnp.sum(lamv[2:3] * lamv[3:4], axis=1, keepdims=True)) + lam_init)
    a1 = acc_ref[0]
    a2 = acc_ref[1]
    o = a1[:, 0:LANES] / a1[:, LANES:2 * LANES] - lam * (a2[:, 0:LANES] / a2[:, LANES:2 * LANES])
    y = o * lax.rsqrt(jnp.mean(o * o, axis=-1, keepdims=True) + NORM_EPS)
    o_ref[0] = (y * subw_ref[...] * (1.0 - lam_init)).astype(BF16)


def _diff_attention(proj3, lamv, subln_w, lam_init):
    bsz, seq, _ = proj3.shape
    tq = min(ATT_TQ, seq)
    tk = min(ATT_TK, seq)
    slopes = jnp.exp2(-ALIBI_MAX_BIAS * jnp.arange(1, ATT_HEADS + 1, dtype=F32) / ATT_HEADS)
    kern = functools.partial(_attn_kernel, tq=tq, tk=tk, seq=seq, lam_init=lam_init)
    nq = C_QA // LANES
    nk = C_KA // LANES
    nv = C_VA // LANES
    return pl.pallas_call(
        kern,
        out_shape=jax.ShapeDtypeStruct((bsz, seq, ATT_HEADS * LANES), BF16),
        grid_spec=pltpu.PrefetchScalarGridSpec(
            num_scalar_prefetch=1,
            grid=(bsz, ATT_HEADS, seq // tq),
            in_specs=[pl.BlockSpec((1, tq, LANES), lambda b, h, i, s: (b, i, nq + h)),
                      pl.BlockSpec((1, seq, LANES), lambda b, h, i, s: (b, 0, nk + h)),
                      pl.BlockSpec((1, seq, LANES), lambda b, h, i, s: (b, 0, nv + h)),
                      pl.BlockSpec((4, LANES), lambda b, h, i, s: (0, 0)),
                      pl.BlockSpec((1, LANES), lambda b, h, i, s: (0, 0))],
            out_specs=pl.BlockSpec((1, tq, LANES), lambda b, h, i, s: (b, i, h)),
            scratch_shapes=[pltpu.VMEM((2, seq, LANES), BF16),
                            pltpu.VMEM((seq, 2 * LANES), BF16),
                            pltpu.VMEM((2, tq, LANES), BF16),
                            pltpu.VMEM((2, tq, LANES), F32),
                            pltpu.VMEM((2, tq, 2 * LANES), F32)]),
        compiler_params=_cparams(("parallel", "arbitrary", "arbitrary")),
        name="diff_attn",
    )(slopes, proj3, proj3, proj3, lamv, subln_w)


def _gdn_kernel(q_ref, k_ref, v_ref, z_ref, small_ref, cwq_ref, cwk_ref, cwv_ref,
                alog_ref, dtb_ref, nw_ref, o_ref, xbuf_ref, pre_ref, state_ref, *, heads, chunk):
    s_idx = pl.program_id(1)
    width = heads * LANES
    pad = 8

    @pl.when(s_idx == 0)
    def _():
        xbuf_ref[:, 0:pad, :] = jnp.zeros((3, pad, width), F32)
        pre_ref[...] = jnp.zeros(pre_ref.shape, F32)
        state_ref[...] = jnp.zeros(state_ref.shape, F32)

    def conv_silu(idx, raw_ref, cw_ref, sl, anchor):
        xbuf_ref[idx, pad:pad + chunk, sl] = raw_ref[0, :, sl].astype(F32)
        cw = cw_ref[:, sl] + anchor
        acc = xbuf_ref[idx, pad:pad + chunk, sl] * cw[CONV_WIDTH - 1:CONV_WIDTH]
        for back in range(1, CONV_WIDTH):
            tap = CONV_WIDTH - 1 - back
            acc = acc + xbuf_ref[idx, pad - back:pad - back + chunk, sl] * cw[tap:tap + 1]
        xbuf_ref[idx, 0:pad, sl] = xbuf_ref[idx, chunk:chunk + pad, sl]
        return _silu(acc)

    small = small_ref[0]
    beta_all = _sigmoid(small)
    sp_in = small + dtb_ref[...]
    softplus = jnp.maximum(sp_in, 0.0) + jnp.log(1.0 + jnp.exp(-jnp.abs(sp_in)))
    g_all = -jnp.exp(alog_ref[...]) * softplus
    row = lax.broadcasted_iota(jnp.int32, (chunk, chunk), 0)
    col = lax.broadcasted_iota(jnp.int32, (chunk, chunk), 1)
    incl = col <= row
    strict = col < row
    tri = jnp.where(incl, 1.0, 0.0).astype(F32)
    gc_all = jnp.dot(tri, g_all, preferred_element_type=F32, precision=HIGHEST)
    eye = jnp.where(row == col, 1.0, 0.0).astype(F32)

    level_masks = []
    for k in range(chunk.bit_length() - 1):
        same = (row >> (k + 1)) == (col >> (k + 1))
        level_masks.append(same & (((row >> k) & 1) == 1) & (((col >> k) & 1) == 0))

    nw = nw_ref[...]
    hs = range(heads)
    sls = [slice(hh * LANES, (hh + 1) * LANES) for hh in hs]

    def prepare(hh, anchor):
        sl = sls[hh]
        qh = conv_silu(0, q_ref, cwq_ref, sl, anchor)
        kh = conv_silu(1, k_ref, cwk_ref, sl, anchor)
        pre_ref[0, :, sl] = qh * lax.rsqrt(jnp.sum(qh * qh, axis=-1, keepdims=True) + NORM_EPS) * (GDN_DK ** -0.5)
        pre_ref[1, :, sl] = kh * lax.rsqrt(jnp.sum(kh * kh, axis=-1, keepdims=True) + NORM_EPS)
        pre_ref[2, :, sl] = conv_silu(2, v_ref, cwv_ref, sl, anchor)

    todo = list(hs)

    def prepare_some(count, after):
        anchor = after[0:CONV_WIDTH, 0:LANES] * 0.0
        for _ in range(min(count, len(todo))):
            prepare(todo.pop(0), anchor)

    qn = [pre_ref[0, :, sls[hh]] for hh in hs]
    kn = [pre_ref[1, :, sls[hh]] for hh in hs]
    vc = [pre_ref[2, :, sls[hh]] for hh in hs]
    kb, gcs, egs, decay = [], [], [], []
    for hh in hs:
        kb.append(kn[hh] * beta_all[:, hh:hh + 1])
        gc = gc_all[:, heads + hh:heads + hh + 1]
        gcs.append(gc)
        egs.append(jnp.exp(gc))
        g_col = jnp.broadcast_to(gc, (chunk, chunk))
        decay.append(jnp.where(incl, jnp.exp(jnp.where(incl, g_col - g_col.T, 0.0)), 0.0))
    kn16 = [kn[hh].astype(BF16) for hh in hs]
    kk = [_dot_nt(kb[hh].astype(BF16), kn16[hh]) for hh in hs]
    qk = [_dot_nt(qn[hh].astype(BF16), kn16[hh]) for hh in hs]
    low = [jnp.where(strict, kk[hh] * decay[hh], 0.0) for hh in hs]
    intra = [(qk[hh] * decay[hh]).astype(BF16) for hh in hs]
    inv = [eye - jnp.where(level_masks[0], low[hh], 0.0) for hh in hs]
    for m in level_masks[1:]:
        inv16 = [inv[hh].astype(BF16) for hh in hs]
        t = [_dot(jnp.where(m, low[hh], 0.0).astype(BF16), inv16[hh]).astype(BF16) for hh in hs]
        inv = [inv[hh] - _dot(inv16[hh], t[hh]) for hh in hs]
        prepare_some(1, inv[0])
    rhs = [jnp.concatenate([vc[hh] * beta_all[:, hh:hh + 1], kb[hh] * egs[hh]], axis=1).astype(BF16)
           for hh in hs]
    uw = [_dot(inv[hh].astype(BF16), rhs[hh]) for hh in hs]
    prepare_some(heads, uw[0])
    state = [state_ref[hh] for hh in hs]
    state16 = [state[hh].astype(BF16) for hh in hs]
    v_new = [(uw[hh][:, 0:LANES] - _dot(uw[hh][:, LANES:2 * LANES].astype(BF16), state16[hh])).astype(BF16)
             for hh in hs]
    for hh in hs:
        g_last = gcs[hh][chunk - 1:chunk, :]
        kdec = kn[hh] * jnp.exp(g_last - gcs[hh])
        state_ref[hh] = state[hh] * jnp.exp(g_last) + _dot(kdec.T.astype(BF16), v_new[hh])
    for hh in hs:
        o = _dot((qn[hh] * egs[hh]).astype(BF16), state16[hh]) + _dot(intra[hh], v_new[hh])
        y = o * lax.rsqrt(jnp.mean(o * o, axis=-1, keepdims=True) + NORM_EPS) * nw
        zf = z_ref[0, :, sls[hh]].astype(F32)
        o_ref[0, :, sls[hh]] = (y * _silu(zf)).astype(BF16)


def _gated_deltanet(proj3, small3, conv_w, a_log, dt_bias, norm_w):
    bsz, seq, _ = proj3.shape
    heads = GDN_HEADS
    width = heads * LANES
    chunk = min(GDN_CHUNK, seq)
    zeros = jnp.zeros((LANES - 2 * heads,), F32)
    alog_row = jnp.concatenate([jnp.zeros((heads,), F32), a_log, zeros]).reshape(1, LANES)
    dtb_row = jnp.concatenate([jnp.zeros((heads,), F32), dt_bias, zeros]).reshape(1, LANES)
    kern = functools.partial(_gdn_kernel, heads=heads, chunk=chunk)
    n_chunks = seq // chunk
    nxt = lambda s: jnp.minimum(s, n_chunks - 1)
    cur = lambda s: jnp.maximum(s - 1, 0)
    raw = lambda off: pl.BlockSpec((1, chunk, width), lambda b, s: (b, nxt(s), off // width))
    cw = lambda j: pl.BlockSpec((CONV_WIDTH, width), lambda b, s: (0, j))
    row = pl.BlockSpec((1, LANES), lambda b, s: (0, 0))
    return pl.pallas_call(
        kern,
        out_shape=jax.ShapeDtypeStruct((bsz, seq, width), BF16),
        grid=(bsz, n_chunks + 1),
        in_specs=[raw(C_QB), raw(C_KB), raw(C_VB),
                  pl.BlockSpec((1, chunk, width), lambda b, s: (b, cur(s), C_ZB // width)),
                  pl.BlockSpec((1, chunk, LANES), lambda b, s: (b, cur(s), 0)),
                  cw(0), cw(1), cw(2), row, row, row],
        out_specs=pl.BlockSpec((1, chunk, width), lambda b, s: (b, cur(s), 0)),
        scratch_shapes=[pltpu.VMEM((3, chunk + 8, width), F32),
                        pltpu.VMEM((3, chunk, width), F32),
                        pltpu.VMEM((heads, GDN_DK, LANES), F32)],
        compiler_params=_cparams(("parallel", "arbitrary")),
        name="gated_deltanet",
    )(proj3, proj3, proj3, proj3, small3, conv_w, conv_w, conv_w, alog_row, dtb_row, norm_w)


def _pack_bf16_pairs(x):
    k = x.shape[1] // 2
    r = pltpu.bitcast(x.astype(BF16).astype(F32), jnp.uint32)
    return (r[:, k:] & jnp.uint32(0xFFFF0000)) | (r[:, :k] >> 16)


def _unpack_bf16_pairs(p):
    lo = pltpu.bitcast(p << 16, F32)
    hi = pltpu.bitcast(p & jnp.uint32(0xFFFF0000), F32)
    return jnp.concatenate([lo, hi], axis=1).astype(BF16)


def _merge_kernel(ya_ref, yb_ref, ga_ref, gb_ref, x_ref, gate1_ref, nw_ref, shift_ref, scale_ref,
                  wa_ref, wb_ref, wo_ref, wr_ref, xo_ref, h2_ref, lg_ref):
    tm = x_ref.shape[0]
    sub = min(MERGE_SUB, tm)
    rows = [slice(i * sub, (i + 1) * sub) for i in range(tm // sub)]
    a = [_dot(ya_ref[r, :], wa_ref[...]) for r in rows]
    b = [_dot(yb_ref[r, :], wb_ref[...]) for r in rows]
    mixed = [(_sigmoid(ga_ref[r, :].astype(F32)) * a[i] + _sigmoid(gb_ref[r, :].astype(F32)) * b[i]).astype(BF16)
             for i, r in enumerate(rows)]
    upd = [_dot(m, wo_ref[...]) for m in mixed]
    for i, r in enumerate(rows):
        xn = x_ref[r, :] + gate1_ref[0] * upd[i]
        xo_ref[r, :] = xn
        ms = jnp.mean(xn * xn, axis=-1, keepdims=True)
        h2 = xn * lax.rsqrt(ms + NORM_EPS) * nw_ref[...]
        h2 = h2 * (1.0 + scale_ref[0]) + shift_ref[0]
        h2_ref[r, :] = _pack_bf16_pairs(h2)
        lg_ref[r, :] = _dot_split(h2, wr_ref)


def _merge_out(ya, yb, proj, x2, gate1, norm_w, shift, scale, wa, wb, wo, wr, seq):
    n, d = x2.shape
    tm = min(512, seq)
    per_b = seq // tm
    g0 = C_GATES // d
    tok = lambda j: pl.BlockSpec((tm, d), lambda i: (i, j))
    per_batch = pl.BlockSpec((1, 1, d), lambda i: (i // per_b, 0, 0))
    full = lambda r, c: pl.BlockSpec((r, c), lambda i: (0, 0))
    return pl.pallas_call(
        _merge_kernel,
        out_shape=(jax.ShapeDtypeStruct((n, d), F32),
                   jax.ShapeDtypeStruct((n, d // 2), jnp.uint32),
                   jax.ShapeDtypeStruct((n, LANES), F32)),
        grid=(n // tm,),
        in_specs=[tok(0), tok(0), tok(g0), tok(g0 + 1), tok(0), per_batch, full(1, d), per_batch, per_batch,
                  full(d, d), full(d, d), full(d, d), full(d, 2 * LANES)],
        out_specs=(tok(0), pl.BlockSpec((tm, d // 2), lambda i: (i, 0)),
                   pl.BlockSpec((tm, LANES), lambda i: (i, 0))),
        compiler_params=_cparams(("parallel",)),
        name="merge_out",
    )(ya, yb, proj, proj, x2, gate1, norm_w, shift, scale, wa, wb, wo, wr)


def _dispatch_kernel(d0_ref, d1_ref, h2_ref, xs_in_ref, xs_ref, sem, *, tile):
    del xs_in_ref
    d_refs = (d0_ref, d1_ref)

    def row_copy(t, k):
        return pltpu.make_async_copy(h2_ref.at[pl.ds(t, 1), :],
                                     xs_ref.at[pl.ds(d_refs[k][t], 1), :], sem)

    def start(t, carry):
        for k in range(TOP_K):
            row_copy(t, k).start()
        return carry

    def wait(t, carry):
        for k in range(TOP_K):
            row_copy(t, k).wait()
        return carry

    lax.fori_loop(0, tile, start, 0, unroll=DMA_UNROLL)
    lax.fori_loop(0, tile, wait, 0, unroll=DMA_UNROLL)


def _dispatch(dests, h2p, n_rows):
    n, half = h2p.shape
    tile = min(DISPATCH_TILE, n)
    xs0 = jnp.zeros((n_rows, half), jnp.uint32)
    kern = functools.partial(_dispatch_kernel, tile=tile)
    idx = pl.BlockSpec((tile,), lambda i: (i,), memory_space=pltpu.SMEM)
    return pl.pallas_call(
        kern,
        out_shape=jax.ShapeDtypeStruct((n_rows, half), jnp.uint32),
        grid=(n // tile,),
        in_specs=[idx, idx,
                  pl.BlockSpec((tile, half), lambda i: (i, 0)),
                  pl.BlockSpec(memory_space=pl.ANY)],
        out_specs=pl.BlockSpec(memory_space=pl.ANY),
        scratch_shapes=[pltpu.SemaphoreType.DMA],
        input_output_aliases={3: 0},
        compiler_params=_cparams(("arbitrary",)),
        name="moe_dispatch",
    )(dests[0], dests[1], h2p, xs0)


def _expert_kernel(be_ref, used_ref, x_ref, w1_ref, w3_ref, w2_ref, y_ref, w1b_ref, w3b_ref, w2b_ref):
    i = pl.program_id(0)
    live = i < used_ref[0]
    new_expert = jnp.logical_or(i == 0, be_ref[i] != be_ref[jnp.maximum(i - 1, 0)])

    @pl.when(new_expert)
    def _():
        w1b_ref[...] = w1_ref[0, 0].astype(BF16)
        w3b_ref[...] = w3_ref[0, 0].astype(BF16)
        w2b_ref[...] = w2_ref[0, 0].astype(BF16)

    @pl.when(live)
    def _():
        x = _unpack_bf16_pairs(x_ref[...])
        a = _dot(x, w1b_ref[...])
        g = _dot(x, w3b_ref[...])
        y_ref[...] = _dot((_silu(a) * g).astype(BF16), w2b_ref[...])

    @pl.when(jnp.logical_not(live))
    def _():
        y_ref[...] = jnp.zeros(y_ref.shape, F32)


def _experts(block_e, used, xs, w1, w3, w2, layer):
    n_rows, half = xs.shape
    d = 2 * half
    n_blocks = n_rows // MOE_ROWS
    live = lambda i, be, used: jnp.minimum(i, used[0] - 1)
    wspec = lambda r, c: pl.BlockSpec((1, 1, r, c), lambda i, be, used: (layer, be[i], 0, 0))
    return pl.pallas_call(
        _expert_kernel,
        out_shape=jax.ShapeDtypeStruct((n_rows, d), F32),
        grid_spec=pltpu.PrefetchScalarGridSpec(
            num_scalar_prefetch=2,
            grid=(n_blocks,),
            in_specs=[pl.BlockSpec((MOE_ROWS, half), lambda i, be, used: (live(i, be, used), 0)),
                      wspec(d, D_EXPERT), wspec(d, D_EXPERT), wspec(D_EXPERT, d)],
            out_specs=pl.BlockSpec((MOE_ROWS, d), lambda i, be, used: (i, 0)),
            scratch_shapes=[pltpu.VMEM((d, D_EXPERT), BF16), pltpu.VMEM((d, D_EXPERT), BF16),
                            pltpu.VMEM((D_EXPERT, d), BF16)]),
        compiler_params=_cparams(("arbitrary",)),
        name="moe_experts",
    )(block_e, used, xs, w1, w3, w2)


def _combine_kernel(c0_ref, c1_ref, n0_ref, n1_ref, ys_ref, x_ref, wt_ref, gate2_ref, fw_ref, o_ref, buf_ref,
                    sems, *, tile, final):
    i = pl.program_id(0)
    slot = i & 1
    cur = (c0_ref, c1_ref)
    nxt = (n0_ref, n1_ref)

    def row_copy(d_refs, t, k, s):
        return pltpu.make_async_copy(ys_ref.at[pl.ds(d_refs[k][t], 1), :],
                                     buf_ref.at[s, k, pl.ds(t, 1), :], sems.at[s])

    def gather(d_refs, s):
        def start(t, carry):
            for k in range(TOP_K):
                row_copy(d_refs, t, k, s).start()
            return carry

        lax.fori_loop(0, tile, start, 0, unroll=DMA_UNROLL)

    @pl.when(i == 0)
    def _():
        gather(cur, 0)

    @pl.when(i < pl.num_programs(0) - 1)
    def _():
        gather(nxt, 1 - slot)

    def wait(t, carry):
        for k in range(TOP_K):
            row_copy(cur, t, k, slot).wait()
        return carry

    lax.fori_loop(0, tile, wait, 0, unroll=DMA_UNROLL)
    wt = wt_ref[...]
    y = wt[:, 0:1] * buf_ref[slot, 0] + wt[:, 1:2] * buf_ref[slot, 1]
    xn = x_ref[...] + gate2_ref[0] * y
    if final:
        xn = xn * lax.rsqrt(jnp.mean(xn * xn, axis=-1, keepdims=True) + NORM_EPS) * fw_ref[...]
    o_ref[...] = xn


def _combine(dests, ys, x2, wts, gate2, final_w, seq, final):
    n, d = x2.shape
    tile = min(TOK_TILE, seq)
    per_b = seq // tile
    steps = n // tile
    kern = functools.partial(_combine_kernel, tile=tile, final=final)
    cur = pl.BlockSpec((tile,), lambda i: (i,), memory_space=pltpu.SMEM)
    nxt = pl.BlockSpec((tile,), lambda i: (jnp.minimum(i + 1, steps - 1),), memory_space=pltpu.SMEM)
    return pl.pallas_call(
        kern,
        out_shape=jax.ShapeDtypeStruct((n, d), F32),
        grid=(steps,),
        in_specs=[cur, cur, nxt, nxt,
                  pl.BlockSpec(memory_space=pl.ANY),
                  pl.BlockSpec((tile, d), lambda i: (i, 0)),
                  pl.BlockSpec((tile, LANES), lambda i: (i, 0)),
                  pl.BlockSpec((1, 1, d), lambda i: (i // per_b, 0, 0)),
                  pl.BlockSpec((1, d), lambda i: (0, 0))],
        out_specs=pl.BlockSpec((tile, d), lambda i: (i, 0)),
        scratch_shapes=[pltpu.VMEM((2, TOP_K, tile, d), F32), pltpu.SemaphoreType.DMA((2,))],
        compiler_params=_cparams(("arbitrary",)),
        name="moe_combine",
    )(dests[0], dests[1], dests[0], dests[1], ys, x2, wts, gate2, final_w)


def _route_kernel(lg_ref, bias_ref, dest_ref, wt_ref, cnt_ref, tri_ref, base_ref, start_ref, *, tl):
    phase = pl.program_id(0)
    i = pl.program_id(1)
    ne = N_EXPERTS
    eg = EXPERTS_PER_GROUP

    @pl.when(jnp.logical_and(phase == 0, i == 0))
    def _():
        base_ref[...] = jnp.zeros(base_ref.shape, F32)
        r = lax.broadcasted_iota(jnp.int32, (tl, tl), 0)
        c = lax.broadcasted_iota(jnp.int32, (tl, tl), 1)
        tri_ref[...] = jnp.where(r < c, 1.0, 0.0).astype(BF16)

    lt = lg_ref[...].T + bias_ref[...]
    gl = lt[0:N_GROUPS]
    gmax = jnp.max(gl, axis=0, keepdims=True)
    gsum = jnp.sum(jnp.exp(gl - gmax), axis=0, keepdims=True)
    grow = lax.broadcasted_iota(jnp.int32, (N_GROUPS, tl), 0)
    gsel = jnp.min(jnp.where(gl == gmax, grow, N_GROUPS), axis=0, keepdims=True)
    gweight = 1.0 / gsum
    el = lt[8:8 + ne]
    sel = el[0:eg]
    for g in range(1, N_GROUPS):
        sel = jnp.where(gsel == g, el[g * eg:(g + 1) * eg], sel)
    erow = lax.broadcasted_iota(jnp.int32, (eg, tl), 0)
    v1 = jnp.max(sel, axis=0, keepdims=True)
    i1 = jnp.min(jnp.where(sel == v1, erow, eg), axis=0, keepdims=True)
    rest = jnp.where(erow == i1, NEG_INF, sel)
    v2 = jnp.max(rest, axis=0, keepdims=True)
    i2 = jnp.min(jnp.where(rest == v2, erow, eg), axis=0, keepdims=True)
    esum = jnp.sum(jnp.exp(sel - v1), axis=0, keepdims=True)
    p1 = 1.0 / esum
    p2 = jnp.exp(v2 - v1) / esum
    w1 = gweight * p1 / (p1 + p2)
    w2 = gweight * p2 / (p1 + p2)
    eio = lax.broadcasted_iota(jnp.int32, (ne, tl), 0)
    oh1 = jnp.where(eio == gsel * eg + i1, 1.0, 0.0)
    oh2 = jnp.where(eio == gsel * eg + i2, 1.0, 0.0)
    oh = oh1 + oh2
    tile_cnt = jnp.sum(oh, axis=1, keepdims=True)

    @pl.when(phase == 0)
    def _():
        base_ref[...] = base_ref[...] + tile_cnt

    @pl.when(jnp.logical_and(phase == 1, i == 0))
    def _():
        cnt = base_ref[...]
        cnt_ref[...] = cnt
        padded = jnp.floor((cnt + (MOE_ROWS - 1)) * (1.0 / MOE_ROWS)) * MOE_ROWS
        r = lax.broadcasted_iota(jnp.int32, (ne, ne), 0)
        c = lax.broadcasted_iota(jnp.int32, (ne, ne), 1)
        below = jnp.where(c < r, 1.0, 0.0).astype(F32)
        start_ref[...] = jnp.dot(below, padded, preferred_element_type=F32, precision=HIGHEST)
        base_ref[...] = jnp.zeros(base_ref.shape, F32)

    @pl.when(phase == 1)
    def _():
        earlier = _dot(oh.astype(BF16), tri_ref[...]) + base_ref[:, 0:1]
        pos = earlier + start_ref[:, 0:1]
        d1 = jnp.sum(oh1 * pos, axis=0, keepdims=True)
        d2 = jnp.sum(oh2 * pos, axis=0, keepdims=True)
        row8 = lax.broadcasted_iota(jnp.int32, (8, tl), 0)
        dest_ref[...] = jnp.where(row8 == 0, d1, jnp.where(row8 == 1, d2, 0.0)).astype(jnp.int32)
        rowl = lax.broadcasted_iota(jnp.int32, (LANES, tl), 0)
        wt_ref[...] = jnp.where(rowl == 0, w1, jnp.where(rowl == 1, w2, 0.0)).T
        base_ref[...] = base_ref[...] + tile_cnt


def _route(logits, b_rg, b_re):
    n = logits.shape[0]
    tl = min(ROUTE_TILE, n)
    bias = jnp.zeros((LANES,), F32).at[0:N_GROUPS].set(b_rg).at[8:8 + N_EXPERTS].set(b_re).reshape(LANES, 1)
    kern = functools.partial(_route_kernel, tl=tl)
    dest, wts, cnt = pl.pallas_call(
        kern,
        out_shape=(jax.ShapeDtypeStruct((8, n), jnp.int32), jax.ShapeDtypeStruct((n, LANES), F32),
                   jax.ShapeDtypeStruct((N_EXPERTS, LANES), F32)),
        grid=(2, n // tl),
        in_specs=[pl.BlockSpec((tl, LANES), lambda p, i: (i, 0)),
                  pl.BlockSpec((LANES, 1), lambda p, i: (0, 0))],
        out_specs=(pl.BlockSpec((8, tl), lambda p, i: (0, i * p)),
                   pl.BlockSpec((tl, LANES), lambda p, i: (i * p, 0)),
                   pl.BlockSpec((N_EXPERTS, LANES), lambda p, i: (0, 0))),
        scratch_shapes=[pltpu.VMEM((tl, tl), BF16), pltpu.VMEM((N_EXPERTS, LANES), F32),
                        pltpu.VMEM((N_EXPERTS, LANES), F32)],
        compiler_params=_cparams(("arbitrary", "arbitrary")),
        name="moe_route",
    )(logits, bias)
    counts = cnt[:, 0].astype(jnp.int32)
    padded = (counts + MOE_ROWS - 1) // MOE_ROWS * MOE_ROWS
    pad_end = jnp.cumsum(padded)
    n_blocks = (n * TOP_K) // MOE_ROWS + N_EXPERTS
    used = jnp.maximum(pad_end[-1] // MOE_ROWS, 1).astype(jnp.int32)
    blk = jnp.minimum(jnp.arange(n_blocks, dtype=jnp.int32), used - 1)
    block_e = jnp.sum((pad_end[None, :] <= (blk * MOE_ROWS)[:, None]).astype(jnp.int32), axis=1)
    block_e = jnp.minimum(block_e, N_EXPERTS - 1).astype(jnp.int32)
    return (dest[0], dest[1]), wts, block_e, used.reshape(1), n_blocks * MOE_ROWS


def kernel(x, c, ada_w, ada_b, norm1_w, w_in, conv_w, lambda_q1, lambda_k1, lambda_q2, lambda_k2,
           subln_w, a_log, dt_bias, gdn_norm_w, w_branch_a, w_branch_b, w_out, norm2_w,
           router_group_w, router_group_b, router_expert_w, router_expert_b,
           expert_w1, expert_w3, expert_w2, final_norm_w):
    bsz, seq, d = x.shape
    depth = ada_w.shape[0]
    n = bsz * seq
    mod = _ada_mod(c, ada_w, ada_b)
    x2 = x.reshape(n, d)
    n_beta = 7168
    for l in range(depth):
        lam_init = 0.8 - 0.6 * math.exp(-0.3 * l)
        m = mod[l].reshape(bsz, 1, 6, d)
        shift1, scale1, gate1, shift2, scale2, gate2 = (m[:, :, i] for i in range(6))
        w = w_in[l]
        w_main = jnp.concatenate([w[:, :n_beta], w[:, n_beta + 2 * GDN_HEADS:]], axis=1).astype(BF16)
        w_small = _split_hi_lo(jnp.pad(w[:, n_beta:n_beta + 2 * GDN_HEADS], ((0, 0), (0, LANES - 2 * GDN_HEADS))))
        proj, small = _in_proj(x2, shift1, scale1, norm1_w[l].reshape(1, d), w_main, w_small, seq)
        proj3 = proj.reshape(bsz, seq, P_MAIN)
        lamv = jnp.pad(jnp.stack([lambda_q1[l], lambda_k1[l], lambda_q2[l], lambda_k2[l]]),
                       ((0, 0), (0, LANES - ATT_HEAD_DIM)))
        ya = _diff_attention(proj3, lamv, subln_w[l].reshape(1, LANES), lam_init)
        yb = _gated_deltanet(proj3, small.reshape(bsz, seq, LANES), conv_w[l], a_log[l], dt_bias[l],
                             gdn_norm_w[l].reshape(1, LANES))
        wr = jnp.zeros((d, LANES), F32)
        wr = wr.at[:, 0:N_GROUPS].set(router_group_w[l]).at[:, 8:8 + N_EXPERTS].set(router_expert_w[l])
        x2, h2p, logits = _merge_out(
            ya.reshape(n, d), yb.reshape(n, d), proj, x2, gate1, norm2_w[l].reshape(1, d), shift2, scale2,
            w_branch_a[l].astype(BF16), w_branch_b[l].astype(BF16), w_out[l].astype(BF16), _split_hi_lo(wr), seq)
        dest, wts, block_e, used, n_rows = _route(logits, router_group_b[l], router_expert_b[l])
        xs = _dispatch(dest, h2p, n_rows)
        ys = _experts(block_e, used, xs, expert_w1, expert_w3, expert_w2, l)
        x2 = _combine(dest, ys, x2, wts, gate2, final_norm_w.reshape(1, d), seq, final=(l == depth - 1))
    return x2.reshape(bsz, seq, d)
```

```python
import functools
import math

import jax
import jax.numpy as jnp
import numpy as np
from jax import lax
from jax.experimental import pallas as pl
from jax.experimental.pallas import tpu as pltpu

F32 = jnp.float32
BF16 = jnp.bfloat16
HIGHEST = lax.Precision.HIGHEST

D_MODEL = 1024
LANES = 128
ATT_HEADS = 8
ATT_HEAD_DIM = 64
ALIBI_MAX_BIAS = 8.0
GDN_HEADS = 8
GDN_DK = 128
CONV_WIDTH = 4
GDN_CHUNK = 128
N_GROUPS = 4
EXPERTS_PER_GROUP = 8
N_EXPERTS = N_GROUPS * EXPERTS_PER_GROUP
TOP_K = 2
D_EXPERT = D_MODEL // 2
NORM_EPS = 1e-6
NEG_INF = -1e30

C_QA, C_KA, C_VA = 0, 1024, 2048
C_QB, C_KB, C_VB = 3072, 4096, 5120
C_ZB = 6144
C_GATES = 7168
P_MAIN = 9216

ATT_TQ = 1024
ATT_TK = 512
ATT_ACC_ROWS = 144
MERGE_SUB = 256
MOE_ROWS = 256
TOK_TILE = 256
DISPATCH_TILE = 512
DMA_UNROLL = 8
ROUTE_TILE = 2048
VMEM_LIMIT = 56 * 1024 * 1024


def _cparams(sem):
    return pltpu.CompilerParams(dimension_semantics=sem, vmem_limit_bytes=VMEM_LIMIT)


def _sigmoid(x):
    return 1.0 / (1.0 + jnp.exp(-x))


def _silu(x):
    return x * _sigmoid(x)


def _dot(a, b):
    return jnp.dot(a, b, preferred_element_type=F32)


def _split_hi_lo(w):
    hi = w.astype(BF16)
    lo = (w - hi.astype(F32)).astype(BF16)
    return jnp.concatenate([hi, lo], axis=1)


def _dot_split(x, w2_ref):
    hi = x.astype(BF16)
    lo = (x - hi.astype(F32)).astype(BF16)
    r = _dot(hi, w2_ref[...])
    return r[:, 0:LANES] + r[:, LANES:2 * LANES] + _dot(lo, w2_ref[:, 0:LANES])


def _dot_nt(a, b):
    return lax.dot_general(a, b, (((1,), (1,)), ((), ())), preferred_element_type=F32)


def _ada_kernel(c_ref, w_ref, b_ref, o_ref):
    cond = _silu(c_ref[...])
    o_ref[0] = jnp.dot(cond, w_ref[0], preferred_element_type=F32, precision=HIGHEST) + b_ref[0]


def _ada_mod(c, ada_w, ada_b):
    depth, d, d6 = ada_w.shape
    bsz = c.shape[0]
    tn = 1536
    return pl.pallas_call(
        _ada_kernel,
        out_shape=jax.ShapeDtypeStruct((depth, bsz, d6), F32),
        grid=(depth, d6 // tn),
        in_specs=[pl.BlockSpec((bsz, d), lambda l, j: (0, 0)),
                  pl.BlockSpec((1, d, tn), lambda l, j: (l, 0, j)),
                  pl.BlockSpec((1, 1, tn), lambda l, j: (l, 0, j))],
        out_specs=pl.BlockSpec((1, bsz, tn), lambda l, j: (l, 0, j)),
        compiler_params=_cparams(("arbitrary", "arbitrary")),
        name="ada_mod",
    )(c, ada_w, ada_b.reshape(depth, 1, d6))


def _inproj_kernel(x_ref, shift_ref, scale_ref, nw_ref, w_ref, ws_ref, o_ref, os_ref, hn_ref):
    @pl.when(pl.program_id(1) == 0)
    def _():
        x = x_ref[...]
        ms = jnp.mean(x * x, axis=-1, keepdims=True)
        hn = x * lax.rsqrt(ms + NORM_EPS) * nw_ref[...]
        hn = hn * (1.0 + scale_ref[0]) + shift_ref[0]
        hn_ref[...] = hn.astype(BF16)
        os_ref[...] = _dot_split(hn, ws_ref)

    o_ref[...] = _dot(hn_ref[...], w_ref[...]).astype(BF16)


def _in_proj(x2, shift, scale, norm_w, w_main, w_small, seq):
    n, d = x2.shape
    tm = min(1024, seq)
    tn = 3072
    per_b = seq // tm
    return pl.pallas_call(
        _inproj_kernel,
        out_shape=(jax.ShapeDtypeStruct((n, P_MAIN), BF16), jax.ShapeDtypeStruct((n, LANES), F32)),
        grid=(n // tm, P_MAIN // tn),
        in_specs=[pl.BlockSpec((tm, d), lambda i, j: (i, 0)),
                  pl.BlockSpec((1, 1, d), lambda i, j: (i // per_b, 0, 0)),
                  pl.BlockSpec((1, 1, d), lambda i, j: (i // per_b, 0, 0)),
                  pl.BlockSpec((1, d), lambda i, j: (0, 0)),
                  pl.BlockSpec((d, tn), lambda i, j: (0, j)),
                  pl.BlockSpec((d, 2 * LANES), lambda i, j: (0, 0))],
        out_specs=(pl.BlockSpec((tm, tn), lambda i, j: (i, j)),
                   pl.BlockSpec((tm, LANES), lambda i, j: (i, 0))),
        scratch_shapes=[pltpu.VMEM((tm, d), BF16)],
        compiler_params=_cparams(("parallel", "arbitrary")),
        name="in_proj",
    )(x2, shift, scale, norm_w, w_main, w_small)


def _bf16_round(x):
    u = int(np.float32(x).view(np.uint32))
    u = (u + 0x7FFF + ((u >> 16) & 1)) & 0xFFFF0000
    return float(np.uint32(u).view(np.float32))


LOG2E = 1.4426950408889634
LOG2E_HI = _bf16_round(LOG2E)
LOG2E_LO = _bf16_round(LOG2E - LOG2E_HI)
N_FEAT = 7


def _feature_lanes(lane, base, values):
    out = jnp.zeros(lane.shape, F32)
    for i, val in enumerate(values):
        out = jnp.where(lane == base + i, val, out)
    return out


def _attn_kernel(slope_ref, q_ref, k_ref, v_ref, lamv_ref, subw_ref, o_ref,
                 ke_ref, vet_ref, qet_ref, m_ref, acc_ref, *, tq, tk, seq, lam_init):
    h = pl.program_id(1)
    qi = pl.program_id(2)
    slope = slope_ref[h]
    hd = ATT_HEAD_DIM
    prep = 512 if seq % 512 == 0 else seq
    extra = ATT_ACC_ROWS - LANES

    @pl.when(qi == 0)
    def _():
        def fill(i, carry):
            r0 = pl.multiple_of(i * prep, prep)
            kf = k_ref[0, pl.ds(r0, prep), :].astype(F32)
            lane = lax.broadcasted_iota(jnp.int32, (prep, LANES), 1)
            j = r0 + lax.broadcasted_iota(jnp.int32, (prep, LANES), 0)
            f_lo = (j & (LANES - 1)).astype(F32) * slope
            f_hi = (j >> (LANES.bit_length() - 1)).astype(F32) * (slope * LANES)
            one = jnp.ones((prep, LANES), F32)
            vals = (f_lo, f_lo, f_hi, f_hi, one, one, one)
            ke_ref[0, pl.ds(r0, prep), :] = jnp.where(lane < hd, kf, _feature_lanes(lane, hd, vals)).astype(BF16)
            ke_ref[1, pl.ds(r0, prep), :] = jnp.where(lane >= hd, kf, _feature_lanes(lane, 0, vals)).astype(BF16)
            vet_ref[0:LANES, pl.ds(r0, prep)] = v_ref[0, pl.ds(r0, prep), :].astype(F32).T.astype(BF16)
            row = lax.broadcasted_iota(jnp.int32, (extra, prep), 0)
            vet_ref[LANES:ATT_ACC_ROWS, pl.ds(r0, prep)] = jnp.where(row == 0, 1.0, 0.0).astype(BF16)
            return carry

        lax.fori_loop(0, seq // prep, fill, 0)

    q_start = qi * tq
    lane = lax.broadcasted_iota(jnp.int32, (tq, LANES), 1)
    qf = q_ref[0].astype(F32) * (hd ** -0.5 * LOG2E)
    cq = jnp.full((tq, LANES), 1.0, F32) * (slope * (-LOG2E) * q_start.astype(F32))
    c_hi = cq.astype(BF16).astype(F32)
    c_mid = (cq - c_hi).astype(BF16).astype(F32)
    c_lo = cq - c_hi - c_mid
    qvals = (LOG2E_HI, LOG2E_LO, LOG2E_HI, LOG2E_LO, c_hi, c_mid, c_lo)
    qet_ref[0] = jnp.where(lane < hd, qf, _feature_lanes(lane, hd, qvals)).T.astype(BF16)
    qet_ref[1] = jnp.where(lane >= hd, qf, _feature_lanes(lane, 0, qvals)).T.astype(BF16)
    m_ref[...] = jnp.full(m_ref.shape, NEG_INF, F32)
    acc_ref[...] = jnp.zeros(acc_ref.shape, F32)

    n_full = q_start // tk

    def blocks(js, masked, c0=0, nc=tq):
        starts = [pl.multiple_of(j * tk, tk) for j in js]
        cols = slice(c0, c0 + nc)
        s_all = [[_dot(ke_ref[mi, pl.ds(k0, tk), :], qet_ref[mi, :, cols]) for mi in range(2)] for k0 in starts]
        if any(masked):
            r = lax.broadcasted_iota(jnp.int32, (tk, nc), 0)
            c = lax.broadcasted_iota(jnp.int32, (tk, nc), 1)
            rel = r - c
        for mi in range(2):
            m = m_ref[mi, :, cols]
            acc = acc_ref[mi, :, cols]
            for bi, k0 in enumerate(starts):
                s = s_all[bi][mi]
                if masked[bi]:
                    s = jnp.where(rel <= (q_start + c0 - k0), s, NEG_INF)
                m_new = jnp.maximum(m, jnp.max(s, axis=0, keepdims=True))
                alpha = jnp.exp2(m - m_new)
                p = jnp.exp2(s - m_new[0:1, :]).astype(BF16)
                acc = alpha[0:1, :] * acc + _dot(vet_ref[:, pl.ds(k0, tk)], p)
                m = m_new
            m_ref[mi, :, cols] = m
            acc_ref[mi, :, cols] = acc

    odd = n_full & 1

    @pl.when(odd == 1)
    def _():
        blocks([0], [False])

    def loop_body(p, carry):
        blocks([odd + 2 * p, odd + 2 * p + 1], [False, False])
        return carry

    lax.fori_loop(0, n_full >> 1, loop_body, 0)
    n_diag = max(1, tq // tk)
    for g in range(n_diag):
        blocks([n_full + d for d in range(g + 1)], [False] * g + [True], g * (tq // n_diag), tq // n_diag)

    lamv = lamv_ref[...]
    lam = (jnp.exp(jnp.sum(lamv[0:1] * lamv[1:2], axis=1, keepdims=True))
           - jnp.exp(jnp.sum(lamv[2:3] * lamv[3:4], axis=1, keepdims=True)) + lam_init)
    a1 = acc_ref[0]
    a2 = acc_ref[1]
    o = a1[0:LANES] / a1[LANES:LANES + 1] - lam * (a2[0:LANES] / a2[LANES:LANES + 1])
    y = o * lax.rsqrt(jnp.mean(o * o, axis=0, keepdims=True) + NORM_EPS)
    o_ref[0] = (y * subw_ref[...] * (1.0 - lam_init)).T.astype(BF16)


def _diff_attention(proj3, lamv, subln_w, lam_init):
    bsz, seq, _ = proj3.shape
    tq = min(ATT_TQ, seq)
    tk = min(ATT_TK, seq)
    slopes = jnp.exp2(-ALIBI_MAX_BIAS * jnp.arange(1, ATT_HEADS + 1, dtype=F32) / ATT_HEADS)
    kern = functools.partial(_attn_kernel, tq=tq, tk=tk, seq=seq, lam_init=lam_init)
    nq = C_QA // LANES
    nk = C_KA // LANES
    nv = C_VA // LANES
    return pl.pallas_call(
        kern,
        out_shape=jax.ShapeDtypeStruct((bsz, seq, ATT_HEADS * LANES), BF16),
        grid_spec=pltpu.PrefetchScalarGridSpec(
            num_scalar_prefetch=1,
            grid=(bsz, ATT_HEADS, seq // tq),
            in_specs=[pl.BlockSpec((1, tq, LANES), lambda b, h, i, s: (b, i, nq + h)),
                      pl.BlockSpec((1, seq, LANES), lambda b, h, i, s: (b, 0, nk + h)),
                      pl.BlockSpec((1, seq, LANES), lambda b, h, i, s: (b, 0, nv + h)),
                      pl.BlockSpec((4, LANES), lambda b, h, i, s: (0, 0)),
                      pl.BlockSpec((LANES, 1), lambda b, h, i, s: (0, 0))],
            out_specs=pl.BlockSpec((1, tq, LANES), lambda b, h, i, s: (b, i, h)),
            scratch_shapes=[pltpu.VMEM((2, seq, LANES), BF16),
                            pltpu.VMEM((ATT_ACC_ROWS, seq), BF16),
                            pltpu.VMEM((2, LANES, tq), BF16),
                            pltpu.VMEM((2, 8, tq), F32),
                            pltpu.VMEM((2, ATT_ACC_ROWS, tq), F32)]),
        compiler_params=_cparams(("parallel", "arbitrary", "arbitrary")),
        name="diff_attn",
    )(slopes, proj3, proj3, proj3, lamv, subln_w)


def _gdn_kernel(q_ref, k_ref, v_ref, z_ref, small_ref, cwq_ref, cwk_ref, cwv_ref,
                alog_ref, dtb_ref, nw_ref, o_ref, xbuf_ref, pre_ref, state_ref, *, heads, chunk):
    s_idx = pl.program_id(1)
    width = heads * LANES
    pad = 8

    @pl.when(s_idx == 0)
    def _():
        xbuf_ref[:, 0:pad, :] = jnp.zeros((3, pad, width), F32)
        pre_ref[...] = jnp.zeros(pre_ref.shape, F32)
        state_ref[...] = jnp.zeros(state_ref.shape, F32)

    def conv_silu(idx, raw_ref, cw_ref, sl, anchor):
        xbuf_ref[idx, pad:pad + chunk, sl] = raw_ref[0, :, sl].astype(F32)
        cw = cw_ref[:, sl] + anchor
        acc = xbuf_ref[idx, pad:pad + chunk, sl] * cw[CONV_WIDTH - 1:CONV_WIDTH]
        for back in range(1, CONV_WIDTH):
            tap = CONV_WIDTH - 1 - back
            acc = acc + xbuf_ref[idx, pad - back:pad - back + chunk, sl] * cw[tap:tap + 1]
        xbuf_ref[idx, 0:pad, sl] = xbuf_ref[idx, chunk:chunk + pad, sl]
        return _silu(acc)

    small = small_ref[0]
    beta_all = _sigmoid(small)
    sp_in = small + dtb_ref[...]
    softplus = jnp.maximum(sp_in, 0.0) + jnp.log(1.0 + jnp.exp(-jnp.abs(sp_in)))
    g_all = -jnp.exp(alog_ref[...]) * softplus
    row = lax.broadcasted_iota(jnp.int32, (chunk, chunk), 0)
    col = lax.broadcasted_iota(jnp.int32, (chunk, chunk), 1)
    incl = col <= row
    strict = col < row
    tri = jnp.where(incl, 1.0, 0.0).astype(F32)
    gc_all = jnp.dot(tri, g_all, preferred_element_type=F32, precision=HIGHEST)
    eye = jnp.where(row == col, 1.0, 0.0).astype(F32)

    level_masks = []
    for k in range(chunk.bit_length() - 1):
        same = (row >> (k + 1)) == (col >> (k + 1))
        level_masks.append(same & (((row >> k) & 1) == 1) & (((col >> k) & 1) == 0))

    nw = nw_ref[...]
    hs = range(heads)
    sls = [slice(hh * LANES, (hh + 1) * LANES) for hh in hs]

    def prepare(hh, anchor):
        sl = sls[hh]
        qh = conv_silu(0, q_ref, cwq_ref, sl, anchor)
        kh = conv_silu(1, k_ref, cwk_ref, sl, anchor)
        pre_ref[0, :, sl] = qh * lax.rsqrt(jnp.sum(qh * qh, axis=-1, keepdims=True) + NORM_EPS) * (GDN_DK ** -0.5)
        pre_ref[1, :, sl] = kh * lax.rsqrt(jnp.sum(kh * kh, axis=-1, keepdims=True) + NORM_EPS)
        pre_ref[2, :, sl] = conv_silu(2, v_ref, cwv_ref, sl, anchor)

    todo = list(hs)

    def prepare_some(count, after):
        anchor = after[0:CONV_WIDTH, 0:LANES] * 0.0
        for _ in range(min(count, len(todo))):
            prepare(todo.pop(0), anchor)

    qn = [pre_ref[0, :, sls[hh]] for hh in hs]
    kn = [pre_ref[1, :, sls[hh]] for hh in hs]
    vc = [pre_ref[2, :, sls[hh]] for hh in hs]
    kb, gcs, egs, decay = [], [], [], []
    for hh in hs:
        kb.append(kn[hh] * beta_all[:, hh:hh + 1])
        gc = gc_all[:, heads + hh:heads + hh + 1]
        gcs.append(gc)
        egs.append(jnp.exp(gc))
        g_col = jnp.broadcast_to(gc, (chunk, chunk))
        decay.append(jnp.where(incl, jnp.exp(jnp.where(incl, g_col - g_col.T, 0.0)), 0.0))
    kn16 = [kn[hh].astype(BF16) for hh in hs]
    kk = [_dot_nt(kb[hh].astype(BF16), kn16[hh]) for hh in hs]
    qk = [_dot_nt(qn[hh].astype(BF16), kn16[hh]) for hh in hs]
    low = [jnp.where(strict, kk[hh] * decay[hh], 0.0) for hh in hs]
    intra = [(qk[hh] * decay[hh]).astype(BF16) for hh in hs]
    inv = [eye - jnp.where(level_masks[0], low[hh], 0.0) for hh in hs]
    for m in level_masks[1:]:
        inv16 = [inv[hh].astype(BF16) for hh in hs]
        t = [_dot(jnp.where(m, low[hh], 0.0).astype(BF16), inv16[hh]).astype(BF16) for hh in hs]
        inv = [inv[hh] - _dot(inv16[hh], t[hh]) for hh in hs]
        prepare_some(1, inv[0])
    rhs = [jnp.concatenate([vc[hh] * beta_all[:, hh:hh + 1], kb[hh] * egs[hh]], axis=1).astype(BF16)
           for hh in hs]
    uw = [_dot(inv[hh].astype(BF16), rhs[hh]) for hh in hs]
    prepare_some(heads, uw[0])
    state = [state_ref[hh] for hh in hs]
    state16 = [state[hh].astype(BF16) for hh in hs]
    v_new = [(uw[hh][:, 0:LANES] - _dot(uw[hh][:, LANES:2 * LANES].astype(BF16), state16[hh])).astype(BF16)
             for hh in hs]
    for hh in hs:
        g_last = gcs[hh][chunk - 1:chunk, :]
        kdec = kn[hh] * jnp.exp(g_last - gcs[hh])
        state_ref[hh] = state[hh] * jnp.exp(g_last) + _dot(kdec.T.astype(BF16), v_new[hh])
    for hh in hs:
        o = _dot((qn[hh] * egs[hh]).astype(BF16), state16[hh]) + _dot(intra[hh], v_new[hh])
        y = o * lax.rsqrt(jnp.mean(o * o, axis=-1, keepdims=True) + NORM_EPS) * nw
        zf = z_ref[0, :, sls[hh]].astype(F32)
        o_ref[0, :, sls[hh]] = (y * _silu(zf)).astype(BF16)


def _gated_deltanet(proj3, small3, conv_w, a_log, dt_bias, norm_w):
    bsz, seq, _ = proj3.shape
    heads = GDN_HEADS
    width = heads * LANES
    chunk = min(GDN_CHUNK, seq)
    zeros = jnp.zeros((LANES - 2 * heads,), F32)
    alog_row = jnp.concatenate([jnp.zeros((heads,), F32), a_log, zeros]).reshape(1, LANES)
    dtb_row = jnp.concatenate([jnp.zeros((heads,), F32), dt_bias, zeros]).reshape(1, LANES)
    kern = functools.partial(_gdn_kernel, heads=heads, chunk=chunk)
    n_chunks = seq // chunk
    nxt = lambda s: jnp.minimum(s, n_chunks - 1)
    cur = lambda s: jnp.maximum(s - 1, 0)
    raw = lambda off: pl.BlockSpec((1, chunk, width), lambda b, s: (b, nxt(s), off // width))
    cw = lambda j: pl.BlockSpec((CONV_WIDTH, width), lambda b, s: (0, j))
    row = pl.BlockSpec((1, LANES), lambda b, s: (0, 0))
    return pl.pallas_call(
        kern,
        out_shape=jax.ShapeDtypeStruct((bsz, seq, width), BF16),
        grid=(bsz, n_chunks + 1),
        in_specs=[raw(C_QB), raw(C_KB), raw(C_VB),
                  pl.BlockSpec((1, chunk, width), lambda b, s: (b, cur(s), C_ZB // width)),
                  pl.BlockSpec((1, chunk, LANES), lambda b, s: (b, cur(s), 0)),
                  cw(0), cw(1), cw(2), row, row, row],
        out_specs=pl.BlockSpec((1, chunk, width), lambda b, s: (b, cur(s), 0)),
        scratch_shapes=[pltpu.VMEM((3, chunk + 8, width), F32),
                        pltpu.VMEM((3, chunk, width), F32),
                        pltpu.VMEM((heads, GDN_DK, LANES), F32)],
        compiler_params=_cparams(("parallel", "arbitrary")),
        name="gated_deltanet",
    )(proj3, proj3, proj3, proj3, small3, conv_w, conv_w, conv_w, alog_row, dtb_row, norm_w)


def _pack_bf16_pairs(x):
    k = x.shape[1] // 2
    r = pltpu.bitcast(x.astype(BF16).astype(F32), jnp.uint32)
    return (r[:, k:] & jnp.uint32(0xFFFF0000)) | (r[:, :k] >> 16)


def _unpack_bf16_pairs(p):
    lo = pltpu.bitcast(p << 16, F32)
    hi = pltpu.bitcast(p & jnp.uint32(0xFFFF0000), F32)
    return jnp.concatenate([lo, hi], axis=1).astype(BF16)


def _merge_kernel(ya_ref, yb_ref, ga_ref, gb_ref, x_ref, gate1_ref, nw_ref, shift_ref, scale_ref,
                  wa_ref, wb_ref, wo_ref, wr_ref, xo_ref, h2_ref, lg_ref):
    tm = x_ref.shape[0]
    sub = min(MERGE_SUB, tm)
    rows = [slice(i * sub, (i + 1) * sub) for i in range(tm // sub)]
    a = [_dot(ya_ref[r, :], wa_ref[...]) for r in rows]
    b = [_dot(yb_ref[r, :], wb_ref[...]) for r in rows]
    mixed = [(_sigmoid(ga_ref[r, :].astype(F32)) * a[i] + _sigmoid(gb_ref[r, :].astype(F32)) * b[i]).astype(BF16)
             for i, r in enumerate(rows)]
    upd = [_dot(m, wo_ref[...]) for m in mixed]
    for i, r in enumerate(rows):
        xn = x_ref[r, :] + gate1_ref[0] * upd[i]
        xo_ref[r, :] = xn
        ms = jnp.mean(xn * xn, axis=-1, keepdims=True)
        h2 = xn * lax.rsqrt(ms + NORM_EPS) * nw_ref[...]
        h2 = h2 * (1.0 + scale_ref[0]) + shift_ref[0]
        h2_ref[r, :] = _pack_bf16_pairs(h2)
        lg_ref[r, :] = _dot_split(h2, wr_ref)


def _merge_out(ya, yb, proj, x2, gate1, norm_w, shift, scale, wa, wb, wo, wr, seq):
    n, d = x2.shape
    tm = min(512, seq)
    per_b = seq // tm
    g0 = C_GATES // d
    tok = lambda j: pl.BlockSpec((tm, d), lambda i: (i, j))
    per_batch = pl.BlockSpec((1, 1, d), lambda i: (i // per_b, 0, 0))
    full = lambda r, c: pl.BlockSpec((r, c), lambda i: (0, 0))
    return pl.pallas_call(
        _merge_kernel,
        out_shape=(jax.ShapeDtypeStruct((n, d), F32),
                   jax.ShapeDtypeStruct((n, d // 2), jnp.uint32),
                   jax.ShapeDtypeStruct((n, LANES), F32)),
        grid=(n // tm,),
        in_specs=[tok(0), tok(0), tok(g0), tok(g0 + 1), tok(0), per_batch, full(1, d), per_batch, per_batch,
                  full(d, d), full(d, d), full(d, d), full(d, 2 * LANES)],
        out_specs=(tok(0), pl.BlockSpec((tm, d // 2), lambda i: (i, 0)),
                   pl.BlockSpec((tm, LANES), lambda i: (i, 0))),
        compiler_params=_cparams(("parallel",)),
        name="merge_out",
    )(ya, yb, proj, proj, x2, gate1, norm_w, shift, scale, wa, wb, wo, wr)


def _dispatch_kernel(d0_ref, d1_ref, h2_ref, xs_in_ref, xs_ref, sem, *, tile):
    del xs_in_ref
    d_refs = (d0_ref, d1_ref)

    def row_copy(t, k):
        return pltpu.make_async_copy(h2_ref.at[pl.ds(t, 1), :],
                                     xs_ref.at[pl.ds(d_refs[k][t], 1), :], sem)

    def start(t, carry):
        for k in range(TOP_K):
            row_copy(t, k).start()
        return carry

    def wait(t, carry):
        for k in range(TOP_K):
            row_copy(t, k).wait()
        return carry

    lax.fori_loop(0, tile, start, 0, unroll=DMA_UNROLL)
    lax.fori_loop(0, tile, wait, 0, unroll=DMA_UNROLL)


def _dispatch(dests, h2p, n_rows):
    n, half = h2p.shape
    tile = min(DISPATCH_TILE, n)
    xs0 = jnp.zeros((n_rows, half), jnp.uint32)
    kern = functools.partial(_dispatch_kernel, tile=tile)
    idx = pl.BlockSpec((tile,), lambda i: (i,), memory_space=pltpu.SMEM)
    return pl.pallas_call(
        kern,
        out_shape=jax.ShapeDtypeStruct((n_rows, half), jnp.uint32),
        grid=(n // tile,),
        in_specs=[idx, idx,
                  pl.BlockSpec((tile, half), lambda i: (i, 0)),
                  pl.BlockSpec(memory_space=pl.ANY)],
        out_specs=pl.BlockSpec(memory_space=pl.ANY),
        scratch_shapes=[pltpu.SemaphoreType.DMA],
        input_output_aliases={3: 0},
        compiler_params=_cparams(("arbitrary",)),
        name="moe_dispatch",
    )(dests[0], dests[1], h2p, xs0)


def _expert_kernel(be_ref, used_ref, x_ref, w1_ref, w3_ref, w2_ref, y_ref, w1b_ref, w3b_ref, w2b_ref):
    i = pl.program_id(0)
    live = i < used_ref[0]
    new_expert = jnp.logical_or(i == 0, be_ref[i] != be_ref[jnp.maximum(i - 1, 0)])

    @pl.when(new_expert)
    def _():
        w1b_ref[...] = w1_ref[0, 0].astype(BF16)
        w3b_ref[...] = w3_ref[0, 0].astype(BF16)
        w2b_ref[...] = w2_ref[0, 0].astype(BF16)

    @pl.when(live)
    def _():
        x = _unpack_bf16_pairs(x_ref[...])
        a = _dot(x, w1b_ref[...])
        g = _dot(x, w3b_ref[...])
        y_ref[...] = _dot((_silu(a) * g).astype(BF16), w2b_ref[...])

    @pl.when(jnp.logical_not(live))
    def _():
        y_ref[...] = jnp.zeros(y_ref.shape, F32)


def _experts(block_e, used, xs, w1, w3, w2, layer):
    n_rows, half = xs.shape
    d = 2 * half
    n_blocks = n_rows // MOE_ROWS
    live = lambda i, be, used: jnp.minimum(i, used[0] - 1)
    wspec = lambda r, c: pl.BlockSpec((1, 1, r, c), lambda i, be, used: (layer, be[i], 0, 0))
    return pl.pallas_call(
        _expert_kernel,
        out_shape=jax.ShapeDtypeStruct((n_rows, d), F32),
        grid_spec=pltpu.PrefetchScalarGridSpec(
            num_scalar_prefetch=2,
            grid=(n_blocks,),
            in_specs=[pl.BlockSpec((MOE_ROWS, half), lambda i, be, used: (live(i, be, used), 0)),
                      wspec(d, D_EXPERT), wspec(d, D_EXPERT), wspec(D_EXPERT, d)],
            out_specs=pl.BlockSpec((MOE_ROWS, d), lambda i, be, used: (i, 0)),
            scratch_shapes=[pltpu.VMEM((d, D_EXPERT), BF16), pltpu.VMEM((d, D_EXPERT), BF16),
                            pltpu.VMEM((D_EXPERT, d), BF16)]),
        compiler_params=_cparams(("arbitrary",)),
        name="moe_experts",
    )(block_e, used, xs, w1, w3, w2)


def _combine_kernel(c0_ref, c1_ref, n0_ref, n1_ref, ys_ref, x_ref, wt_ref, gate2_ref, fw_ref, o_ref, buf_ref,
                    sems, *, tile, final):
    i = pl.program_id(0)
    slot = i & 1
    cur = (c0_ref, c1_ref)
    nxt = (n0_ref, n1_ref)

    def row_copy(d_refs, t, k, s):
        return pltpu.make_async_copy(ys_ref.at[pl.ds(d_refs[k][t], 1), :],
                                     buf_ref.at[s, k, pl.ds(t, 1), :], sems.at[s])

    def gather(d_refs, s):
        def start(t, carry):
            for k in range(TOP_K):
                row_copy(d_refs, t, k, s).start()
            return carry

        lax.fori_loop(0, tile, start, 0, unroll=DMA_UNROLL)

    @pl.when(i == 0)
    def _():
        gather(cur, 0)

    @pl.when(i < pl.num_programs(0) - 1)
    def _():
        gather(nxt, 1 - slot)

    def wait(t, carry):
        for k in range(TOP_K):
            row_copy(cur, t, k, slot).wait()
        return carry

    lax.fori_loop(0, tile, wait, 0, unroll=DMA_UNROLL)
    wt = wt_ref[...]
    y = wt[:, 0:1] * buf_ref[slot, 0] + wt[:, 1:2] * buf_ref[slot, 1]
    xn = x_ref[...] + gate2_ref[0] * y
    if final:
        xn = xn * lax.rsqrt(jnp.mean(xn * xn, axis=-1, keepdims=True) + NORM_EPS) * fw_ref[...]
    o_ref[...] = xn


def _combine(dests, ys, x2, wts, gate2, final_w, seq, final):
    n, d = x2.shape
    tile = min(TOK_TILE, seq)
    per_b = seq // tile
    steps = n // tile
    kern = functools.partial(_combine_kernel, tile=tile, final=final)
    cur = pl.BlockSpec((tile,), lambda i: (i,), memory_space=pltpu.SMEM)
    nxt = pl.BlockSpec((tile,), lambda i: (jnp.minimum(i + 1, steps - 1),), memory_space=pltpu.SMEM)
    return pl.pallas_call(
        kern,
        out_shape=jax.ShapeDtypeStruct((n, d), F32),
        grid=(steps,),
        in_specs=[cur, cur, nxt, nxt,
                  pl.BlockSpec(memory_space=pl.ANY),
                  pl.BlockSpec((tile, d), lambda i: (i, 0)),
                  pl.BlockSpec((tile, LANES), lambda i: (i, 0)),
                  pl.BlockSpec((1, 1, d), lambda i: (i // per_b, 0, 0)),
                  pl.BlockSpec((1, d), lambda i: (0, 0))],
        out_specs=pl.BlockSpec((tile, d), lambda i: (i, 0)),
        scratch_shapes=[pltpu.VMEM((2, TOP_K, tile, d), F32), pltpu.SemaphoreType.DMA((2,))],
        compiler_params=_cparams(("arbitrary",)),
        name="moe_combine",
    )(dests[0], dests[1], dests[0], dests[1], ys, x2, wts, gate2, final_w)


def _route_kernel(lg_ref, bias_ref, dest_ref, wt_ref, cnt_ref, tri_ref, base_ref, start_ref, *, tl):
    phase = pl.program_id(0)
    i = pl.program_id(1)
    ne = N_EXPERTS
    eg = EXPERTS_PER_GROUP

    @pl.when(jnp.logical_and(phase == 0, i == 0))
    def _():
        base_ref[...] = jnp.zeros(base_ref.shape, F32)
        r = lax.broadcasted_iota(jnp.int32, (tl, tl), 0)
        c = lax.broadcasted_iota(jnp.int32, (tl, tl), 1)
        tri_ref[...] = jnp.where(r < c, 1.0, 0.0).astype(BF16)

    lt = lg_ref[...].T + bias_ref[...]
    gl = lt[0:N_GROUPS]
    gmax = jnp.max(gl, axis=0, keepdims=True)
    gsum = jnp.sum(jnp.exp(gl - gmax), axis=0, keepdims=True)
    grow = lax.broadcasted_iota(jnp.int32, (N_GROUPS, tl), 0)
    gsel = jnp.min(jnp.where(gl == gmax, grow, N_GROUPS), axis=0, keepdims=True)
    gweight = 1.0 / gsum
    el = lt[8:8 + ne]
    sel = el[0:eg]
    for g in range(1, N_GROUPS):
        sel = jnp.where(gsel == g, el[g * eg:(g + 1) * eg], sel)
    erow = lax.broadcasted_iota(jnp.int32, (eg, tl), 0)
    v1 = jnp.max(sel, axis=0, keepdims=True)
    i1 = jnp.min(jnp.where(sel == v1, erow, eg), axis=0, keepdims=True)
    rest = jnp.where(erow == i1, NEG_INF, sel)
    v2 = jnp.max(rest, axis=0, keepdims=True)
    i2 = jnp.min(jnp.where(rest == v2, erow, eg), axis=0, keepdims=True)
    esum = jnp.sum(jnp.exp(sel - v1), axis=0, keepdims=True)
    p1 = 1.0 / esum
    p2 = jnp.exp(v2 - v1) / esum
    w1 = gweight * p1 / (p1 + p2)
    w2 = gweight * p2 / (p1 + p2)
    eio = lax.broadcasted_iota(jnp.int32, (ne, tl), 0)
    oh1 = jnp.where(eio == gsel * eg + i1, 1.0, 0.0)
    oh2 = jnp.where(eio == gsel * eg + i2, 1.0, 0.0)
    oh = oh1 + oh2
    tile_cnt = jnp.sum(oh, axis=1, keepdims=True)

    @pl.when(phase == 0)
    def _():
        base_ref[...] = base_ref[...] + tile_cnt

    @pl.when(jnp.logical_and(phase == 1, i == 0))
    def _():
        cnt = base_ref[...]
        cnt_ref[...] = cnt
        padded = jnp.floor((cnt + (MOE_ROWS - 1)) * (1.0 / MOE_ROWS)) * MOE_ROWS
        r = lax.broadcasted_iota(jnp.int32, (ne, ne), 0)
        c = lax.broadcasted_iota(jnp.int32, (ne, ne), 1)
        below = jnp.where(c < r, 1.0, 0.0).astype(F32)
        start_ref[...] = jnp.dot(below, padded, preferred_element_type=F32, precision=HIGHEST)
        base_ref[...] = jnp.zeros(base_ref.shape, F32)

    @pl.when(phase == 1)
    def _():
        earlier = _dot(oh.astype(BF16), tri_ref[...]) + base_ref[:, 0:1]
        pos = earlier + start_ref[:, 0:1]
        d1 = jnp.sum(oh1 * pos, axis=0, keepdims=True)
        d2 = jnp.sum(oh2 * pos, axis=0, keepdims=True)
        row8 = lax.broadcasted_iota(jnp.int32, (8, tl), 0)
        dest_ref[...] = jnp.where(row8 == 0, d1, jnp.where(row8 == 1, d2, 0.0)).astype(jnp.int32)
        rowl = lax.broadcasted_iota(jnp.int32, (LANES, tl), 0)
        wt_ref[...] = jnp.where(rowl == 0, w1, jnp.where(rowl == 1, w2, 0.0)).T
        base_ref[...] = base_ref[...] + tile_cnt


def _route(logits, b_rg, b_re):
    n = logits.shape[0]
    tl = min(ROUTE_TILE, n)
    bias = jnp.zeros((LANES,), F32).at[0:N_GROUPS].set(b_rg).at[8:8 + N_EXPERTS].set(b_re).reshape(LANES, 1)
    kern = functools.partial(_route_kernel, tl=tl)
    dest, wts, cnt = pl.pallas_call(
        kern,
        out_shape=(jax.ShapeDtypeStruct((8, n), jnp.int32), jax.ShapeDtypeStruct((n, LANES), F32),
                   jax.ShapeDtypeStruct((N_EXPERTS, LANES), F32)),
        grid=(2, n // tl),
        in_specs=[pl.BlockSpec((tl, LANES), lambda p, i: (i, 0)),
                  pl.BlockSpec((LANES, 1), lambda p, i: (0, 0))],
        out_specs=(pl.BlockSpec((8, tl), lambda p, i: (0, i * p)),
                   pl.BlockSpec((tl, LANES), lambda p, i: (i * p, 0)),
                   pl.BlockSpec((N_EXPERTS, LANES), lambda p, i: (0, 0))),
        scratch_shapes=[pltpu.VMEM((tl, tl), BF16), pltpu.VMEM((N_EXPERTS, LANES), F32),
                        pltpu.VMEM((N_EXPERTS, LANES), F32)],
        compiler_params=_cparams(("arbitrary", "arbitrary")),
        name="moe_route",
    )(logits, bias)
    counts = cnt[:, 0].astype(jnp.int32)
    padded = (counts + MOE_ROWS - 1) // MOE_ROWS * MOE_ROWS
    pad_end = jnp.cumsum(padded)
    n_blocks = (n * TOP_K) // MOE_ROWS + N_EXPERTS
    used = jnp.maximum(pad_end[-1] // MOE_ROWS, 1).astype(jnp.int32)
    blk = jnp.minimum(jnp.arange(n_blocks, dtype=jnp.int32), used - 1)
    block_e = jnp.sum((pad_end[None, :] <= (blk * MOE_ROWS)[:, None]).astype(jnp.int32), axis=1)
    block_e = jnp.minimum(block_e, N_EXPERTS - 1).astype(jnp.int32)
    return (dest[0], dest[1]), wts, block_e, used.reshape(1), n_blocks * MOE_ROWS


def kernel(x, c, ada_w, ada_b, norm1_w, w_in, conv_w, lambda_q1, lambda_k1, lambda_q2, lambda_k2,
           subln_w, a_log, dt_bias, gdn_norm_w, w_branch_a, w_branch_b, w_out, norm2_w,
           router_group_w, router_group_b, router_expert_w, router_expert_b,
           expert_w1, expert_w3, expert_w2, final_norm_w):
    bsz, seq, d = x.shape
    depth = ada_w.shape[0]
    n = bsz * seq
    mod = _ada_mod(c, ada_w, ada_b)
    x2 = x.reshape(n, d)
    n_beta = 7168
    for l in range(depth):
        lam_init = 0.8 - 0.6 * math.exp(-0.3 * l)
        m = mod[l].reshape(bsz, 1, 6, d)
        shift1, scale1, gate1, shift2, scale2, gate2 = (m[:, :, i] for i in range(6))
        w = w_in[l]
        w_main = jnp.concatenate([w[:, :n_beta], w[:, n_beta + 2 * GDN_HEADS:]], axis=1).astype(BF16)
        w_small = _split_hi_lo(jnp.pad(w[:, n_beta:n_beta + 2 * GDN_HEADS], ((0, 0), (0, LANES - 2 * GDN_HEADS))))
        proj, small = _in_proj(x2, shift1, scale1, norm1_w[l].reshape(1, d), w_main, w_small, seq)
        proj3 = proj.reshape(bsz, seq, P_MAIN)
        lamv = jnp.pad(jnp.stack([lambda_q1[l], lambda_k1[l], lambda_q2[l], lambda_k2[l]]),
                       ((0, 0), (0, LANES - ATT_HEAD_DIM)))
        ya = _diff_attention(proj3, lamv, subln_w[l].reshape(LANES, 1), lam_init)
        yb = _gated_deltanet(proj3, small.reshape(bsz, seq, LANES), conv_w[l], a_log[l], dt_bias[l],
                             gdn_norm_w[l].reshape(1, LANES))
        wr = jnp.zeros((d, LANES), F32)
        wr = wr.at[:, 0:N_GROUPS].set(router_group_w[l]).at[:, 8:8 + N_EXPERTS].set(router_expert_w[l])
        x2, h2p, logits = _merge_out(
            ya.reshape(n, d), yb.reshape(n, d), proj, x2, gate1, norm2_w[l].reshape(1, d), shift2, scale2,
            w_branch_a[l].astype(BF16), w_branch_b[l].astype(BF16), w_out[l].astype(BF16), _split_hi_lo(wr), seq)
        dest, wts, block_e, used, n_rows = _route(logits, router_group_b[l], router_expert_b[l])
        xs = _dispatch(dest, h2p, n_rows)
        ys = _experts(block_e, used, xs, expert_w1, expert_w3, expert_w2, l)
        x2 = _combine(dest, ys, x2, wts, gate2, final_norm_w.reshape(1, d), seq, final=(l == depth - 1))
    return x2.reshape(bsz, seq, d)
```

```python
import functools
import math

import jax
import jax.numpy as jnp
import numpy as np
from jax import lax
from jax.experimental import pallas as pl
from jax.experimental.pallas import tpu as pltpu

F32 = jnp.float32
BF16 = jnp.bfloat16
HIGHEST = lax.Precision.HIGHEST

D_MODEL = 1024
LANES = 128
ATT_HEADS = 8
ATT_HEAD_DIM = 64
ALIBI_MAX_BIAS = 8.0
GDN_HEADS = 8
GDN_DK = 128
CONV_WIDTH = 4
GDN_CHUNK = 128
N_GROUPS = 4
EXPERTS_PER_GROUP = 8
N_EXPERTS = N_GROUPS * EXPERTS_PER_GROUP
TOP_K = 2
D_EXPERT = D_MODEL // 2
NORM_EPS = 1e-6
NEG_INF = -1e30

C_QA, C_KA, C_VA = 0, 1024, 2048
C_QB, C_KB, C_VB = 3072, 4096, 5120
C_ZB = 6144
C_GATES = 7168
P_MAIN = 9216

ATT_TQ = 2048
ATT_TK = 512
MERGE_SUB = 256
MOE_ROWS = 256
TOK_TILE = 256
DISPATCH_TILE = 512
DMA_UNROLL = 8
ROUTE_TILE = 2048
VMEM_LIMIT = 56 * 1024 * 1024


def _cparams(sem):
    return pltpu.CompilerParams(dimension_semantics=sem, vmem_limit_bytes=VMEM_LIMIT)


def _sigmoid(x):
    return 1.0 / (1.0 + jnp.exp(-x))


def _silu(x):
    return x * _sigmoid(x)


def _dot(a, b):
    return jnp.dot(a, b, preferred_element_type=F32)


def _split_hi_lo(w):
    hi = w.astype(BF16)
    lo = (w - hi.astype(F32)).astype(BF16)
    return jnp.concatenate([hi, lo], axis=1)


def _dot_split(x, w2_ref):
    hi = x.astype(BF16)
    lo = (x - hi.astype(F32)).astype(BF16)
    r = _dot(hi, w2_ref[...])
    return r[:, 0:LANES] + r[:, LANES:2 * LANES] + _dot(lo, w2_ref[:, 0:LANES])


def _dot_nt(a, b):
    return lax.dot_general(a, b, (((1,), (1,)), ((), ())), preferred_element_type=F32)


def _ada_kernel(c_ref, w_ref, b_ref, o_ref):
    cond = _silu(c_ref[...])
    o_ref[0] = jnp.dot(cond, w_ref[0], preferred_element_type=F32, precision=HIGHEST) + b_ref[0]


def _ada_mod(c, ada_w, ada_b):
    depth, d, d6 = ada_w.shape
    bsz = c.shape[0]
    tn = 1536
    return pl.pallas_call(
        _ada_kernel,
        out_shape=jax.ShapeDtypeStruct((depth, bsz, d6), F32),
        grid=(depth, d6 // tn),
        in_specs=[pl.BlockSpec((bsz, d), lambda l, j: (0, 0)),
                  pl.BlockSpec((1, d, tn), lambda l, j: (l, 0, j)),
                  pl.BlockSpec((1, 1, tn), lambda l, j: (l, 0, j))],
        out_specs=pl.BlockSpec((1, bsz, tn), lambda l, j: (l, 0, j)),
        compiler_params=_cparams(("arbitrary", "arbitrary")),
        name="ada_mod",
    )(c, ada_w, ada_b.reshape(depth, 1, d6))


def _inproj_kernel(x_ref, shift_ref, scale_ref, nw_ref, w_ref, ws_ref, o_ref, os_ref, hn_ref):
    @pl.when(pl.program_id(1) == 0)
    def _():
        x = x_ref[...]
        ms = jnp.mean(x * x, axis=-1, keepdims=True)
        hn = x * lax.rsqrt(ms + NORM_EPS) * nw_ref[...]
        hn = hn * (1.0 + scale_ref[0]) + shift_ref[0]
        hn_ref[...] = hn.astype(BF16)
        os_ref[...] = _dot_split(hn, ws_ref)

    o_ref[...] = _dot(hn_ref[...], w_ref[...]).astype(BF16)


def _in_proj(x2, shift, scale, norm_w, w_main, w_small, seq):
    n, d = x2.shape
    tm = min(1024, seq)
    tn = 3072
    per_b = seq // tm
    return pl.pallas_call(
        _inproj_kernel,
        out_shape=(jax.ShapeDtypeStruct((n, P_MAIN), BF16), jax.ShapeDtypeStruct((n, LANES), F32)),
        grid=(n // tm, P_MAIN // tn),
        in_specs=[pl.BlockSpec((tm, d), lambda i, j: (i, 0)),
                  pl.BlockSpec((1, 1, d), lambda i, j: (i // per_b, 0, 0)),
                  pl.BlockSpec((1, 1, d), lambda i, j: (i // per_b, 0, 0)),
                  pl.BlockSpec((1, d), lambda i, j: (0, 0)),
                  pl.BlockSpec((d, tn), lambda i, j: (0, j)),
                  pl.BlockSpec((d, 2 * LANES), lambda i, j: (0, 0))],
        out_specs=(pl.BlockSpec((tm, tn), lambda i, j: (i, j)),
                   pl.BlockSpec((tm, LANES), lambda i, j: (i, 0))),
        scratch_shapes=[pltpu.VMEM((tm, d), BF16)],
        compiler_params=_cparams(("parallel", "arbitrary")),
        name="in_proj",
    )(x2, shift, scale, norm_w, w_main, w_small)


def _bf16_round(x):
    u = int(np.float32(x).view(np.uint32))
    u = (u + 0x7FFF + ((u >> 16) & 1)) & 0xFFFF0000
    return float(np.uint32(u).view(np.float32))


LOG2E = 1.4426950408889634
LOG2E_HI = _bf16_round(LOG2E)
LOG2E_LO = _bf16_round(LOG2E - LOG2E_HI)
N_FEAT = 7


def _feature_lanes(lane, base, values):
    out = jnp.zeros(lane.shape, F32)
    for i, val in enumerate(values):
        out = jnp.where(lane == base + i, val, out)
    return out


def _attn_kernel(slope_ref, q_ref, k_ref, v_ref, lamv_ref, subw_ref, o_ref,
                 ke_ref, ve_ref, qe_ref, m_ref, acc_ref, *, tq, tk, seq, lam_init):
    h = pl.program_id(1)
    qi = pl.program_id(2)
    slope = slope_ref[h]
    hd = ATT_HEAD_DIM
    prep = 512 if seq % 512 == 0 else seq

    @pl.when(qi == 0)
    def _():
        def fill(i, carry):
            r0 = pl.multiple_of(i * prep, prep)
            kf = k_ref[0, pl.ds(r0, prep), :].astype(F32)
            lane = lax.broadcasted_iota(jnp.int32, (prep, LANES), 1)
            j = r0 + lax.broadcasted_iota(jnp.int32, (prep, LANES), 0)
            f_lo = (j & (LANES - 1)).astype(F32) * slope
            f_hi = (j >> (LANES.bit_length() - 1)).astype(F32) * (slope * LANES)
            one = jnp.ones((prep, LANES), F32)
            vals = (f_lo, f_lo, f_hi, f_hi, one, one, one)
            ke_ref[0, pl.ds(r0, prep), :] = jnp.where(lane < hd, kf, _feature_lanes(lane, hd, vals)).astype(BF16)
            ke_ref[1, pl.ds(r0, prep), :] = jnp.where(lane >= hd, kf, _feature_lanes(lane, 0, vals)).astype(BF16)
            ve_ref[pl.ds(r0, prep), 0:LANES] = v_ref[0, pl.ds(r0, prep), :]
            ve_ref[pl.ds(r0, prep), LANES:2 * LANES] = jnp.ones((prep, LANES), BF16)
            return carry

        lax.fori_loop(0, seq // prep, fill, 0)

    q_start = qi * tq
    lane = lax.broadcasted_iota(jnp.int32, (tq, LANES), 1)
    qf = q_ref[0].astype(F32) * (hd ** -0.5 * LOG2E)
    cq = jnp.full((tq, LANES), 1.0, F32) * (slope * (-LOG2E) * q_start.astype(F32))
    c_hi = cq.astype(BF16).astype(F32)
    c_mid = (cq - c_hi).astype(BF16).astype(F32)
    c_lo = cq - c_hi - c_mid
    qvals = (LOG2E_HI, LOG2E_LO, LOG2E_HI, LOG2E_LO, c_hi, c_mid, c_lo)
    qe_ref[0] = jnp.where(lane < hd, qf, _feature_lanes(lane, hd, qvals)).astype(BF16)
    qe_ref[1] = jnp.where(lane >= hd, qf, _feature_lanes(lane, 0, qvals)).astype(BF16)
    m_ref[...] = jnp.full(m_ref.shape, NEG_INF, F32)
    acc_ref[...] = jnp.zeros(acc_ref.shape, F32)

    n_full = q_start // tk

    def blocks(js, masked, r0=0, nr=tq):
        starts = [pl.multiple_of(j * tk, tk) for j in js]
        rows = slice(r0, r0 + nr)
        s_all = [[_dot_nt(qe_ref[mi, rows, :], ke_ref[mi, pl.ds(k0, tk), :]) for mi in range(2)] for k0 in starts]
        if any(masked):
            r = lax.broadcasted_iota(jnp.int32, (nr, tk), 0)
            c = lax.broadcasted_iota(jnp.int32, (nr, tk), 1)
            rel = c - r
        for mi in range(2):
            m = m_ref[mi, rows, :]
            acc = acc_ref[mi, rows, :]
            for bi, k0 in enumerate(starts):
                s = s_all[bi][mi]
                if masked[bi]:
                    s = jnp.where(rel <= (q_start + r0 - k0), s, NEG_INF)
                m_new = jnp.maximum(m, jnp.max(s, axis=1, keepdims=True))
                alpha = jnp.exp2(m - m_new)
                p = jnp.exp2(s - jnp.tile(m_new, (1, tk // LANES))).astype(BF16)
                acc = jnp.tile(alpha, (1, 2)) * acc + _dot(p, ve_ref[pl.ds(k0, tk), :])
                m = m_new
            m_ref[mi, rows, :] = m
            acc_ref[mi, rows, :] = acc

    odd = n_full & 1

    @pl.when(odd == 1)
    def _():
        blocks([0], [False])

    def loop_body(p, carry):
        blocks([odd + 2 * p, odd + 2 * p + 1], [False, False])
        return carry

    lax.fori_loop(0, n_full >> 1, loop_body, 0)
    n_diag = max(1, tq // tk)
    for g in range(n_diag):
        blocks([n_full + d for d in range(g + 1)], [False] * g + [True], g * (tq // n_diag), tq // n_diag)

    lamv = lamv_ref[...]
    lam = (jnp.exp(jnp.sum(lamv[0:1] * lamv[1:2], axis=1, keepdims=True))
           - jnp.exp(jnp.sum(lamv[2:3] * lamv[3:4], axis=1, keepdims=True)) + lam_init)
    a1 = acc_ref[0]
    a2 = acc_ref[1]
    o = a1[:, 0:LANES] / a1[:, LANES:2 * LANES] - lam * (a2[:, 0:LANES] / a2[:, LANES:2 * LANES])
    y = o * lax.rsqrt(jnp.mean(o * o, axis=-1, keepdims=True) + NORM_EPS)
    o_ref[0] = (y * subw_ref[...] * (1.0 - lam_init)).astype(BF16)


def _diff_attention(proj3, lamv, subln_w, lam_init):
    bsz, seq, _ = proj3.shape
    tq = min(ATT_TQ, seq)
    tk = min(ATT_TK, seq)
    slopes = jnp.exp2(-ALIBI_MAX_BIAS * jnp.arange(1, ATT_HEADS + 1, dtype=F32) / ATT_HEADS)
    kern = functools.partial(_attn_kernel, tq=tq, tk=tk, seq=seq, lam_init=lam_init)
    nq = C_QA // LANES
    nk = C_KA // LANES
    nv = C_VA // LANES
    return pl.pallas_call(
        kern,
        out_shape=jax.ShapeDtypeStruct((bsz, seq, ATT_HEADS * LANES), BF16),
        grid_spec=pltpu.PrefetchScalarGridSpec(
            num_scalar_prefetch=1,
            grid=(bsz, ATT_HEADS, seq // tq),
            in_specs=[pl.BlockSpec((1, tq, LANES), lambda b, h, i, s: (b, i, nq + h)),
                      pl.BlockSpec((1, seq, LANES), lambda b, h, i, s: (b, 0, nk + h)),
                      pl.BlockSpec((1, seq, LANES), lambda b, h, i, s: (b, 0, nv + h)),
                      pl.BlockSpec((4, LANES), lambda b, h, i, s: (0, 0)),
                      pl.BlockSpec((1, LANES), lambda b, h, i, s: (0, 0))],
            out_specs=pl.BlockSpec((1, tq, LANES), lambda b, h, i, s: (b, i, h)),
            scratch_shapes=[pltpu.VMEM((2, seq, LANES), BF16),
                            pltpu.VMEM((seq, 2 * LANES), BF16),
                            pltpu.VMEM((2, tq, LANES), BF16),
                            pltpu.VMEM((2, tq, LANES), F32),
                            pltpu.VMEM((2, tq, 2 * LANES), F32)]),
        compiler_params=_cparams(("parallel", "arbitrary", "arbitrary")),
        name="diff_attn",
    )(slopes, proj3, proj3, proj3, lamv, subln_w)


def _gdn_kernel(q_ref, k_ref, v_ref, z_ref, small_ref, cwq_ref, cwk_ref, cwv_ref,
                alog_ref, dtb_ref, nw_ref, o_ref, xbuf_ref, pre_ref, state_ref, *, heads, chunk):
    s_idx = pl.program_id(1)
    width = heads * LANES
    pad = 8

    @pl.when(s_idx == 0)
    def _():
        xbuf_ref[:, 0:pad, :] = jnp.zeros((3, pad, width), F32)
        pre_ref[...] = jnp.zeros(pre_ref.shape, F32)
        state_ref[...] = jnp.zeros(state_ref.shape, F32)

    def conv_silu(idx, raw_ref, cw_ref, sl, anchor):
        xbuf_ref[idx, pad:pad + chunk, sl] = raw_ref[0, :, sl].astype(F32)
        cw = cw_ref[:, sl] + anchor
        acc = xbuf_ref[idx, pad:pad + chunk, sl] * cw[CONV_WIDTH - 1:CONV_WIDTH]
        for back in range(1, CONV_WIDTH):
            tap = CONV_WIDTH - 1 - back
            acc = acc + xbuf_ref[idx, pad - back:pad - back + chunk, sl] * cw[tap:tap + 1]
        xbuf_ref[idx, 0:pad, sl] = xbuf_ref[idx, chunk:chunk + pad, sl]
        return _silu(acc)

    small = small_ref[0]
    beta_all = _sigmoid(small)
    sp_in = small + dtb_ref[...]
    softplus = jnp.maximum(sp_in, 0.0) + jnp.log(1.0 + jnp.exp(-jnp.abs(sp_in)))
    g_all = -jnp.exp(alog_ref[...]) * softplus
    row = lax.broadcasted_iota(jnp.int32, (chunk, chunk), 0)
    col = lax.broadcasted_iota(jnp.int32, (chunk, chunk), 1)
    incl = col <= row
    strict = col < row
    tri = jnp.where(incl, 1.0, 0.0).astype(F32)
    gc_all = jnp.dot(tri, g_all, preferred_element_type=F32, precision=HIGHEST)
    eye = jnp.where(row == col, 1.0, 0.0).astype(F32)

    level_masks = []
    for k in range(chunk.bit_length() - 1):
        same = (row >> (k + 1)) == (col >> (k + 1))
        level_masks.append(same & (((row >> k) & 1) == 1) & (((col >> k) & 1) == 0))

    nw = nw_ref[...]
    hs = range(heads)
    sls = [slice(hh * LANES, (hh + 1) * LANES) for hh in hs]

    def prepare(hh, anchor):
        sl = sls[hh]
        qh = conv_silu(0, q_ref, cwq_ref, sl, anchor)
        kh = conv_silu(1, k_ref, cwk_ref, sl, anchor)
        pre_ref[0, :, sl] = qh * lax.rsqrt(jnp.sum(qh * qh, axis=-1, keepdims=True) + NORM_EPS) * (GDN_DK ** -0.5)
        pre_ref[1, :, sl] = kh * lax.rsqrt(jnp.sum(kh * kh, axis=-1, keepdims=True) + NORM_EPS)
        pre_ref[2, :, sl] = conv_silu(2, v_ref, cwv_ref, sl, anchor)

    todo = list(hs)

    def prepare_some(count, after):
        anchor = after[0:CONV_WIDTH, 0:LANES] * 0.0
        for _ in range(min(count, len(todo))):
            prepare(todo.pop(0), anchor)

    qn = [pre_ref[0, :, sls[hh]] for hh in hs]
    kn = [pre_ref[1, :, sls[hh]] for hh in hs]
    vc = [pre_ref[2, :, sls[hh]] for hh in hs]
    kb, gcs, egs, decay = [], [], [], []
    for hh in hs:
        kb.append(kn[hh] * beta_all[:, hh:hh + 1])
        gc = gc_all[:, heads + hh:heads + hh + 1]
        gcs.append(gc)
        egs.append(jnp.exp(gc))
        g_col = jnp.broadcast_to(gc, (chunk, chunk))
        decay.append(jnp.where(incl, jnp.exp(jnp.where(incl, g_col - g_col.T, 0.0)), 0.0))
    kn16 = [kn[hh].astype(BF16) for hh in hs]
    kk = [_dot_nt(kb[hh].astype(BF16), kn16[hh]) for hh in hs]
    qk = [_dot_nt(qn[hh].astype(BF16), kn16[hh]) for hh in hs]
    low = [jnp.where(strict, kk[hh] * decay[hh], 0.0) for hh in hs]
    intra = [(qk[hh] * decay[hh]).astype(BF16) for hh in hs]
    inv = [eye - jnp.where(level_masks[0], low[hh], 0.0) for hh in hs]
    for m in level_masks[1:]:
        inv16 = [inv[hh].astype(BF16) for hh in hs]
        t = [_dot(jnp.where(m, low[hh], 0.0).astype(BF16), inv16[hh]).astype(BF16) for hh in hs]
        inv = [inv[hh] - _dot(inv16[hh], t[hh]) for hh in hs]
        prepare_some(1, inv[0])
    rhs = [jnp.concatenate([vc[hh] * beta_all[:, hh:hh + 1], kb[hh] * egs[hh]], axis=1).astype(BF16)
           for hh in hs]
    uw = [_dot(inv[hh].astype(BF16), rhs[hh]) for hh in hs]
    prepare_some(heads, uw[0])
    state = [state_ref[hh] for hh in hs]
    state16 = [state[hh].astype(BF16) for hh in hs]
    v_new = [(uw[hh][:, 0:LANES] - _dot(uw[hh][:, LANES:2 * LANES].astype(BF16), state16[hh])).astype(BF16)
             for hh in hs]
    for hh in hs:
        g_last = gcs[hh][chunk - 1:chunk, :]
        kdec = kn[hh] * jnp.exp(g_last - gcs[hh])
        state_ref[hh] = state[hh] * jnp.exp(g_last) + _dot(kdec.T.astype(BF16), v_new[hh])
    for hh in hs:
        o = _dot((qn[hh] * egs[hh]).astype(BF16), state16[hh]) + _dot(intra[hh], v_new[hh])
        y = o * lax.rsqrt(jnp.mean(o * o, axis=-1, keepdims=True) + NORM_EPS) * nw
        zf = z_ref[0, :, sls[hh]].astype(F32)
        o_ref[0, :, sls[hh]] = (y * _silu(zf)).astype(BF16)


def _gated_deltanet(proj3, small3, conv_w, a_log, dt_bias, norm_w):
    bsz, seq, _ = proj3.shape
    heads = GDN_HEADS
    width = heads * LANES
    chunk = min(GDN_CHUNK, seq)
    zeros = jnp.zeros((LANES - 2 * heads,), F32)
    alog_row = jnp.concatenate([jnp.zeros((heads,), F32), a_log, zeros]).reshape(1, LANES)
    dtb_row = jnp.concatenate([jnp.zeros((heads,), F32), dt_bias, zeros]).reshape(1, LANES)
    kern = functools.partial(_gdn_kernel, heads=heads, chunk=chunk)
    n_chunks = seq // chunk
    nxt = lambda s: jnp.minimum(s, n_chunks - 1)
    cur = lambda s: jnp.maximum(s - 1, 0)
    raw = lambda off: pl.BlockSpec((1, chunk, width), lambda b, s: (b, nxt(s), off // width))
    cw = lambda j: pl.BlockSpec((CONV_WIDTH, width), lambda b, s: (0, j))
    row = pl.BlockSpec((1, LANES), lambda b, s: (0, 0))
    return pl.pallas_call(
        kern,
        out_shape=jax.ShapeDtypeStruct((bsz, seq, width), BF16),
        grid=(bsz, n_chunks + 1),
        in_specs=[raw(C_QB), raw(C_KB), raw(C_VB),
                  pl.BlockSpec((1, chunk, width), lambda b, s: (b, cur(s), C_ZB // width)),
                  pl.BlockSpec((1, chunk, LANES), lambda b, s: (b, cur(s), 0)),
                  cw(0), cw(1), cw(2), row, row, row],
        out_specs=pl.BlockSpec((1, chunk, width), lambda b, s: (b, cur(s), 0)),
        scratch_shapes=[pltpu.VMEM((3, chunk + 8, width), F32),
                        pltpu.VMEM((3, chunk, width), F32),
                        pltpu.VMEM((heads, GDN_DK, LANES), F32)],
        compiler_params=_cparams(("parallel", "arbitrary")),
        name="gated_deltanet",
    )(proj3, proj3, proj3, proj3, small3, conv_w, conv_w, conv_w, alog_row, dtb_row, norm_w)


def _pack_bf16_pairs(x):
    k = x.shape[1] // 2
    r = pltpu.bitcast(x.astype(BF16).astype(F32), jnp.uint32)
    return (r[:, k:] & jnp.uint32(0xFFFF0000)) | (r[:, :k] >> 16)


def _unpack_bf16_pairs(p):
    lo = pltpu.bitcast(p << 16, F32)
    hi = pltpu.bitcast(p & jnp.uint32(0xFFFF0000), F32)
    return jnp.concatenate([lo, hi], axis=1).astype(BF16)


def _merge_kernel(ya_ref, yb_ref, ga_ref, gb_ref, x_ref, gate1_ref, nw_ref, shift_ref, scale_ref,
                  wa_ref, wb_ref, wo_ref, wr_ref, xo_ref, h2_ref, lg_ref):
    tm = x_ref.shape[0]
    sub = min(MERGE_SUB, tm)
    rows = [slice(i * sub, (i + 1) * sub) for i in range(tm // sub)]
    a = [_dot(ya_ref[r, :], wa_ref[...]) for r in rows]
    b = [_dot(yb_ref[r, :], wb_ref[...]) for r in rows]
    mixed = [(_sigmoid(ga_ref[r, :].astype(F32)) * a[i] + _sigmoid(gb_ref[r, :].astype(F32)) * b[i]).astype(BF16)
             for i, r in enumerate(rows)]
    upd = [_dot(m, wo_ref[...]) for m in mixed]
    for i, r in enumerate(rows):
        xn = x_ref[r, :] + gate1_ref[0] * upd[i]
        xo_ref[r, :] = xn
        ms = jnp.mean(xn * xn, axis=-1, keepdims=True)
        h2 = xn * lax.rsqrt(ms + NORM_EPS) * nw_ref[...]
        h2 = h2 * (1.0 + scale_ref[0]) + shift_ref[0]
        h2_ref[r, :] = _pack_bf16_pairs(h2)
        lg_ref[r, :] = _dot_split(h2, wr_ref)


def _merge_out(ya, yb, proj, x2, gate1, norm_w, shift, scale, wa, wb, wo, wr, seq):
    n, d = x2.shape
    tm = min(512, seq)
    per_b = seq // tm
    g0 = C_GATES // d
    tok = lambda j: pl.BlockSpec((tm, d), lambda i: (i, j))
    per_batch = pl.BlockSpec((1, 1, d), lambda i: (i // per_b, 0, 0))
    full = lambda r, c: pl.BlockSpec((r, c), lambda i: (0, 0))
    return pl.pallas_call(
        _merge_kernel,
        out_shape=(jax.ShapeDtypeStruct((n, d), F32),
                   jax.ShapeDtypeStruct((n, d // 2), jnp.uint32),
                   jax.ShapeDtypeStruct((n, LANES), F32)),
        grid=(n // tm,),
        in_specs=[tok(0), tok(0), tok(g0), tok(g0 + 1), tok(0), per_batch, full(1, d), per_batch, per_batch,
                  full(d, d), full(d, d), full(d, d), full(d, 2 * LANES)],
        out_specs=(tok(0), pl.BlockSpec((tm, d // 2), lambda i: (i, 0)),
                   pl.BlockSpec((tm, LANES), lambda i: (i, 0))),
        compiler_params=_cparams(("parallel",)),
        name="merge_out",
    )(ya, yb, proj, proj, x2, gate1, norm_w, shift, scale, wa, wb, wo, wr)


def _dispatch_kernel(d0_ref, d1_ref, h2_ref, xs_in_ref, xs_ref, sem, *, tile):
    del xs_in_ref
    d_refs = (d0_ref, d1_ref)

    def row_copy(t, k):
        return pltpu.make_async_copy(h2_ref.at[pl.ds(t, 1), :],
                                     xs_ref.at[pl.ds(d_refs[k][t], 1), :], sem)

    def start(t, carry):
        for k in range(TOP_K):
            row_copy(t, k).start()
        return carry

    def wait(t, carry):
        for k in range(TOP_K):
            row_copy(t, k).wait()
        return carry

    lax.fori_loop(0, tile, start, 0, unroll=DMA_UNROLL)
    lax.fori_loop(0, tile, wait, 0, unroll=DMA_UNROLL)


def _dispatch(dests, h2p, n_rows):
    n, half = h2p.shape
    tile = min(DISPATCH_TILE, n)
    xs0 = jnp.zeros((n_rows, half), jnp.uint32)
    kern = functools.partial(_dispatch_kernel, tile=tile)
    idx = pl.BlockSpec((tile,), lambda i: (i,), memory_space=pltpu.SMEM)
    return pl.pallas_call(
        kern,
        out_shape=jax.ShapeDtypeStruct((n_rows, half), jnp.uint32),
        grid=(n // tile,),
        in_specs=[idx, idx,
                  pl.BlockSpec((tile, half), lambda i: (i, 0)),
                  pl.BlockSpec(memory_space=pl.ANY)],
        out_specs=pl.BlockSpec(memory_space=pl.ANY),
        scratch_shapes=[pltpu.SemaphoreType.DMA],
        input_output_aliases={3: 0},
        compiler_params=_cparams(("arbitrary",)),
        name="moe_dispatch",
    )(dests[0], dests[1], h2p, xs0)


def _expert_kernel(be_ref, used_ref, x_ref, w1_ref, w3_ref, w2_ref, y_ref, w1b_ref, w3b_ref, w2b_ref):
    i = pl.program_id(0)
    live = i < used_ref[0]
    new_expert = jnp.logical_or(i == 0, be_ref[i] != be_ref[jnp.maximum(i - 1, 0)])

    @pl.when(new_expert)
    def _():
        w1b_ref[...] = w1_ref[0, 0].astype(BF16)
        w3b_ref[...] = w3_ref[0, 0].astype(BF16)
        w2b_ref[...] = w2_ref[0, 0].astype(BF16)

    @pl.when(live)
    def _():
        x = _unpack_bf16_pairs(x_ref[...])
        a = _dot(x, w1b_ref[...])
        g = _dot(x, w3b_ref[...])
        y_ref[...] = _dot((_silu(a) * g).astype(BF16), w2b_ref[...])

    @pl.when(jnp.logical_not(live))
    def _():
        y_ref[...] = jnp.zeros(y_ref.shape, F32)


def _experts(block_e, used, xs, w1, w3, w2, layer):
    n_rows, half = xs.shape
    d = 2 * half
    n_blocks = n_rows // MOE_ROWS
    live = lambda i, be, used: jnp.minimum(i, used[0] - 1)
    wspec = lambda r, c: pl.BlockSpec((1, 1, r, c), lambda i, be, used: (layer, be[i], 0, 0))
    return pl.pallas_call(
        _expert_kernel,
        out_shape=jax.ShapeDtypeStruct((n_rows, d), F32),
        grid_spec=pltpu.PrefetchScalarGridSpec(
            num_scalar_prefetch=2,
            grid=(n_blocks,),
            in_specs=[pl.BlockSpec((MOE_ROWS, half), lambda i, be, used: (live(i, be, used), 0)),
                      wspec(d, D_EXPERT), wspec(d, D_EXPERT), wspec(D_EXPERT, d)],
            out_specs=pl.BlockSpec((MOE_ROWS, d), lambda i, be, used: (i, 0)),
            scratch_shapes=[pltpu.VMEM((d, D_EXPERT), BF16), pltpu.VMEM((d, D_EXPERT), BF16),
                            pltpu.VMEM((D_EXPERT, d), BF16)]),
        compiler_params=_cparams(("arbitrary",)),
        name="moe_experts",
    )(block_e, used, xs, w1, w3, w2)


def _combine_kernel(c0_ref, c1_ref, n0_ref, n1_ref, ys_ref, x_ref, wt_ref, gate2_ref, fw_ref, o_ref, buf_ref,
                    sems, *, tile, final):
    i = pl.program_id(0)
    slot = i & 1
    cur = (c0_ref, c1_ref)
    nxt = (n0_ref, n1_ref)

    def row_copy(d_refs, t, k, s):
        return pltpu.make_async_copy(ys_ref.at[pl.ds(d_refs[k][t], 1), :],
                                     buf_ref.at[s, k, pl.ds(t, 1), :], sems.at[s])

    def gather(d_refs, s):
        def start(t, carry):
            for k in range(TOP_K):
                row_copy(d_refs, t, k, s).start()
            return carry

        lax.fori_loop(0, tile, start, 0, unroll=DMA_UNROLL)

    @pl.when(i == 0)
    def _():
        gather(cur, 0)

    @pl.when(i < pl.num_programs(0) - 1)
    def _():
        gather(nxt, 1 - slot)

    def wait(t, carry):
        for k in range(TOP_K):
            row_copy(cur, t, k, slot).wait()
        return carry

    lax.fori_loop(0, tile, wait, 0, unroll=DMA_UNROLL)
    wt = wt_ref[...]
    y = wt[:, 0:1] * buf_ref[slot, 0] + wt[:, 1:2] * buf_ref[slot, 1]
    xn = x_ref[...] + gate2_ref[0] * y
    if final:
        xn = xn * lax.rsqrt(jnp.mean(xn * xn, axis=-1, keepdims=True) + NORM_EPS) * fw_ref[...]
    o_ref[...] = xn


def _combine(dests, ys, x2, wts, gate2, final_w, seq, final):
    n, d = x2.shape
    tile = min(TOK_TILE, seq)
    per_b = seq // tile
    steps = n // tile
    kern = functools.partial(_combine_kernel, tile=tile, final=final)
    cur = pl.BlockSpec((tile,), lambda i: (i,), memory_space=pltpu.SMEM)
    nxt = pl.BlockSpec((tile,), lambda i: (jnp.minimum(i + 1, steps - 1),), memory_space=pltpu.SMEM)
    return pl.pallas_call(
        kern,
        out_shape=jax.ShapeDtypeStruct((n, d), F32),
        grid=(steps,),
        in_specs=[cur, cur, nxt, nxt,
                  pl.BlockSpec(memory_space=pl.ANY),
                  pl.BlockSpec((tile, d), lambda i: (i, 0)),
                  pl.BlockSpec((tile, LANES), lambda i: (i, 0)),
                  pl.BlockSpec((1, 1, d), lambda i: (i // per_b, 0, 0)),
                  pl.BlockSpec((1, d), lambda i: (0, 0))],
        out_specs=pl.BlockSpec((tile, d), lambda i: (i, 0)),
        scratch_shapes=[pltpu.VMEM((2, TOP_K, tile, d), F32), pltpu.SemaphoreType.DMA((2,))],
        compiler_params=_cparams(("arbitrary",)),
        name="moe_combine",
    )(dests[0], dests[1], dests[0], dests[1], ys, x2, wts, gate2, final_w)


def _route_kernel(lg_ref, bias_ref, dest_ref, wt_ref, cnt_ref, tri_ref, base_ref, start_ref, *, tl):
    phase = pl.program_id(0)
    i = pl.program_id(1)
    ne = N_EXPERTS
    eg = EXPERTS_PER_GROUP

    @pl.when(jnp.logical_and(phase == 0, i == 0))
    def _():
        base_ref[...] = jnp.zeros(base_ref.shape, F32)
        r = lax.broadcasted_iota(jnp.int32, (tl, tl), 0)
        c = lax.broadcasted_iota(jnp.int32, (tl, tl), 1)
        tri_ref[...] = jnp.where(r < c, 1.0, 0.0).astype(BF16)

    lt = lg_ref[...].T + bias_ref[...]
    gl = lt[0:N_GROUPS]
    gmax = jnp.max(gl, axis=0, keepdims=True)
    gsum = jnp.sum(jnp.exp(gl - gmax), axis=0, keepdims=True)
    grow = lax.broadcasted_iota(jnp.int32, (N_GROUPS, tl), 0)
    gsel = jnp.min(jnp.where(gl == gmax, grow, N_GROUPS), axis=0, keepdims=True)
    gweight = 1.0 / gsum
    el = lt[8:8 + ne]
    sel = el[0:eg]
    for g in range(1, N_GROUPS):
        sel = jnp.where(gsel == g, el[g * eg:(g + 1) * eg], sel)
    erow = lax.broadcasted_iota(jnp.int32, (eg, tl), 0)
    v1 = jnp.max(sel, axis=0, keepdims=True)
    i1 = jnp.min(jnp.where(sel == v1, erow, eg), axis=0, keepdims=True)
    rest = jnp.where(erow == i1, NEG_INF, sel)
    v2 = jnp.max(rest, axis=0, keepdims=True)
    i2 = jnp.min(jnp.where(rest == v2, erow, eg), axis=0, keepdims=True)
    esum = jnp.sum(jnp.exp(sel - v1), axis=0, keepdims=True)
    p1 = 1.0 / esum
    p2 = jnp.exp(v2 - v1) / esum
    w1 = gweight * p1 / (p1 + p2)
    w2 = gweight * p2 / (p1 + p2)
    eio = lax.broadcasted_iota(jnp.int32, (ne, tl), 0)
    oh1 = jnp.where(eio == gsel * eg + i1, 1.0, 0.0)
    oh2 = jnp.where(eio == gsel * eg + i2, 1.0, 0.0)
    oh = oh1 + oh2
    tile_cnt = jnp.sum(oh, axis=1, keepdims=True)

    @pl.when(phase == 0)
    def _():
        base_ref[...] = base_ref[...] + tile_cnt

    @pl.when(jnp.logical_and(phase == 1, i == 0))
    def _():
        cnt = base_ref[...]
        cnt_ref[...] = cnt
        padded = jnp.floor((cnt + (MOE_ROWS - 1)) * (1.0 / MOE_ROWS)) * MOE_ROWS
        r = lax.broadcasted_iota(jnp.int32, (ne, ne), 0)
        c = lax.broadcasted_iota(jnp.int32, (ne, ne), 1)
        below = jnp.where(c < r, 1.0, 0.0).astype(F32)
        start_ref[...] = jnp.dot(below, padded, preferred_element_type=F32, precision=HIGHEST)
        base_ref[...] = jnp.zeros(base_ref.shape, F32)

    @pl.when(phase == 1)
    def _():
        earlier = _dot(oh.astype(BF16), tri_ref[...]) + base_ref[:, 0:1]
        pos = earlier + start_ref[:, 0:1]
        d1 = jnp.sum(oh1 * pos, axis=0, keepdims=True)
        d2 = jnp.sum(oh2 * pos, axis=0, keepdims=True)
        row8 = lax.broadcasted_iota(jnp.int32, (8, tl), 0)
        dest_ref[...] = jnp.where(row8 == 0, d1, jnp.where(row8 == 1, d2, 0.0)).astype(jnp.int32)
        rowl = lax.broadcasted_iota(jnp.int32, (LANES, tl), 0)
        wt_ref[...] = jnp.where(rowl == 0, w1, jnp.where(rowl == 1, w2, 0.0)).T
        base_ref[...] = base_ref[...] + tile_cnt


def _route(logits, b_rg, b_re):
    n = logits.shape[0]
    tl = min(ROUTE_TILE, n)
    bias = jnp.zeros((LANES,), F32).at[0:N_GROUPS].set(b_rg).at[8:8 + N_EXPERTS].set(b_re).reshape(LANES, 1)
    kern = functools.partial(_route_kernel, tl=tl)
    dest, wts, cnt = pl.pallas_call(
        kern,
        out_shape=(jax.ShapeDtypeStruct((8, n), jnp.int32), jax.ShapeDtypeStruct((n, LANES), F32),
                   jax.ShapeDtypeStruct((N_EXPERTS, LANES), F32)),
        grid=(2, n // tl),
        in_specs=[pl.BlockSpec((tl, LANES), lambda p, i: (i, 0)),
                  pl.BlockSpec((LANES, 1), lambda p, i: (0, 0))],
        out_specs=(pl.BlockSpec((8, tl), lambda p, i: (0, i * p)),
                   pl.BlockSpec((tl, LANES), lambda p, i: (i * p, 0)),
                   pl.BlockSpec((N_EXPERTS, LANES), lambda p, i: (0, 0))),
        scratch_shapes=[pltpu.VMEM((tl, tl), BF16), pltpu.VMEM((N_EXPERTS, LANES), F32),
                        pltpu.VMEM((N_EXPERTS, LANES), F32)],
        compiler_params=_cparams(("arbitrary", "arbitrary")),
        name="moe_route",
    )(logits, bias)
    counts = cnt[:, 0].astype(jnp.int32)
    padded = (counts + MOE_ROWS - 1) // MOE_ROWS * MOE_ROWS
    pad_end = jnp.cumsum(padded)
    n_blocks = (n * TOP_K) // MOE_ROWS + N_EXPERTS
    used = jnp.maximum(pad_end[-1] // MOE_ROWS, 1).astype(jnp.int32)
    blk = jnp.minimum(jnp.arange(n_blocks, dtype=jnp.int32), used - 1)
    block_e = jnp.sum((pad_end[None, :] <= (blk * MOE_ROWS)[:, None]).astype(jnp.int32), axis=1)
    block_e = jnp.minimum(block_e, N_EXPERTS - 1).astype(jnp.int32)
    return (dest[0], dest[1]), wts, block_e, used.reshape(1), n_blocks * MOE_ROWS


def kernel(x, c, ada_w, ada_b, norm1_w, w_in, conv_w, lambda_q1, lambda_k1, lambda_q2, lambda_k2,
           subln_w, a_log, dt_bias, gdn_norm_w, w_branch_a, w_branch_b, w_out, norm2_w,
           router_group_w, router_group_b, router_expert_w, router_expert_b,
           expert_w1, expert_w3, expert_w2, final_norm_w):
    bsz, seq, d = x.shape
    depth = ada_w.shape[0]
    n = bsz * seq
    mod = _ada_mod(c, ada_w, ada_b)
    x2 = x.reshape(n, d)
    n_beta = 7168
    for l in range(depth):
        lam_init = 0.8 - 0.6 * math.exp(-0.3 * l)
        m = mod[l].reshape(bsz, 1, 6, d)
        shift1, scale1, gate1, shift2, scale2, gate2 = (m[:, :, i] for i in range(6))
        w = w_in[l]
        w_main = jnp.concatenate([w[:, :n_beta], w[:, n_beta + 2 * GDN_HEADS:]], axis=1).astype(BF16)
        w_small = _split_hi_lo(jnp.pad(w[:, n_beta:n_beta + 2 * GDN_HEADS], ((0, 0), (0, LANES - 2 * GDN_HEADS))))
        proj, small = _in_proj(x2, shift1, scale1, norm1_w[l].reshape(1, d), w_main, w_small, seq)
        proj3 = proj.reshape(bsz, seq, P_MAIN)
        lamv = jnp.pad(jnp.stack([lambda_q1[l], lambda_k1[l], lambda_q2[l], lambda_k2[l]]),
                       ((0, 0), (0, LANES - ATT_HEAD_DIM)))
        ya = _diff_attention(proj3, lamv, subln_w[l].reshape(1, LANES), lam_init)
        yb = _gated_deltanet(proj3, small.reshape(bsz, seq, LANES), conv_w[l], a_log[l], dt_bias[l],
                             gdn_norm_w[l].reshape(1, LANES))
        wr = jnp.zeros((d, LANES), F32)
        wr = wr.at[:, 0:N_GROUPS].set(router_group_w[l]).at[:, 8:8 + N_EXPERTS].set(router_expert_w[l])
        x2, h2p, logits = _merge_out(
            ya.reshape(n, d), yb.reshape(n, d), proj, x2, gate1, norm2_w[l].reshape(1, d), shift2, scale2,
            w_branch_a[l].astype(BF16), w_branch_b[l].astype(BF16), w_out[l].astype(BF16), _split_hi_lo(wr), seq)
        dest, wts, block_e, used, n_rows = _route(logits, router_group_b[l], router_expert_b[l])
        xs = _dispatch(dest, h2p, n_rows)
        ys = _experts(block_e, used, xs, expert_w1, expert_w3, expert_w2, l)
        x2 = _combine(dest, ys, x2, wts, gate2, final_norm_w.reshape(1, d), seq, final=(l == depth - 1))
    return x2.reshape(bsz, seq, d)
```

```python
import functools
import math

import jax
import jax.numpy as jnp
import numpy as np
from jax import lax
from jax.experimental import pallas as pl
from jax.experimental.pallas import tpu as pltpu

F32 = jnp.float32
BF16 = jnp.bfloat16
HIGHEST = lax.Precision.HIGHEST

D_MODEL = 1024
LANES = 128
ATT_HEADS = 8
ATT_HEAD_DIM = 64
ALIBI_MAX_BIAS = 8.0
GDN_HEADS = 8
GDN_DK = 128
CONV_WIDTH = 4
GDN_CHUNK = 128
N_GROUPS = 4
EXPERTS_PER_GROUP = 8
N_EXPERTS = N_GROUPS * EXPERTS_PER_GROUP
TOP_K = 2
D_EXPERT = D_MODEL // 2
NORM_EPS = 1e-6
NEG_INF = -1e30

C_QA, C_KA, C_VA = 0, 1024, 2048
C_QB, C_KB, C_VB = 3072, 4096, 5120
C_ZB = 6144
C_GATES = 7168
P_MAIN = 9216

ATT_TQ = 2048
ATT_TK = 512
MERGE_SUB = 256
MOE_ROWS = 512
TOK_TILE = 256
DISPATCH_TILE = 512
DMA_UNROLL = 8
ROUTE_TILE = 2048
VMEM_LIMIT = 56 * 1024 * 1024


def _cparams(sem):
    return pltpu.CompilerParams(dimension_semantics=sem, vmem_limit_bytes=VMEM_LIMIT)


def _sigmoid(x):
    return 1.0 / (1.0 + jnp.exp(-x))


def _silu(x):
    return x * _sigmoid(x)


def _dot(a, b):
    return jnp.dot(a, b, preferred_element_type=F32)


def _split_hi_lo(w):
    hi = w.astype(BF16)
    lo = (w - hi.astype(F32)).astype(BF16)
    return jnp.concatenate([hi, lo], axis=1)


def _dot_split(x, w2_ref):
    hi = x.astype(BF16)
    lo = (x - hi.astype(F32)).astype(BF16)
    r = _dot(hi, w2_ref[...])
    return r[:, 0:LANES] + r[:, LANES:2 * LANES] + _dot(lo, w2_ref[:, 0:LANES])


def _dot_nt(a, b):
    return lax.dot_general(a, b, (((1,), (1,)), ((), ())), preferred_element_type=F32)


def _ada_kernel(c_ref, w_ref, b_ref, o_ref):
    cond = _silu(c_ref[...])
    o_ref[0] = jnp.dot(cond, w_ref[0], preferred_element_type=F32, precision=HIGHEST) + b_ref[0]


def _ada_mod(c, ada_w, ada_b):
    depth, d, d6 = ada_w.shape
    bsz = c.shape[0]
    tn = 1536
    return pl.pallas_call(
        _ada_kernel,
        out_shape=jax.ShapeDtypeStruct((depth, bsz, d6), F32),
        grid=(depth, d6 // tn),
        in_specs=[pl.BlockSpec((bsz, d), lambda l, j: (0, 0)),
                  pl.BlockSpec((1, d, tn), lambda l, j: (l, 0, j)),
                  pl.BlockSpec((1, 1, tn), lambda l, j: (l, 0, j))],
        out_specs=pl.BlockSpec((1, bsz, tn), lambda l, j: (l, 0, j)),
        compiler_params=_cparams(("arbitrary", "arbitrary")),
        name="ada_mod",
    )(c, ada_w, ada_b.reshape(depth, 1, d6))


def _inproj_kernel(x_ref, shift_ref, scale_ref, nw_ref, w_ref, ws_ref, o_ref, os_ref, hn_ref):
    @pl.when(pl.program_id(1) == 0)
    def _():
        x = x_ref[...]
        ms = jnp.mean(x * x, axis=-1, keepdims=True)
        hn = x * lax.rsqrt(ms + NORM_EPS) * nw_ref[...]
        hn = hn * (1.0 + scale_ref[0]) + shift_ref[0]
        hn_ref[...] = hn.astype(BF16)
        os_ref[...] = _dot_split(hn, ws_ref)

    o_ref[...] = _dot(hn_ref[...], w_ref[...]).astype(BF16)


def _in_proj(x2, shift, scale, norm_w, w_main, w_small, seq):
    n, d = x2.shape
    tm = min(1024, seq)
    tn = 3072
    per_b = seq // tm
    return pl.pallas_call(
        _inproj_kernel,
        out_shape=(jax.ShapeDtypeStruct((n, P_MAIN), BF16), jax.ShapeDtypeStruct((n, LANES), F32)),
        grid=(n // tm, P_MAIN // tn),
        in_specs=[pl.BlockSpec((tm, d), lambda i, j: (i, 0)),
                  pl.BlockSpec((1, 1, d), lambda i, j: (i // per_b, 0, 0)),
                  pl.BlockSpec((1, 1, d), lambda i, j: (i // per_b, 0, 0)),
                  pl.BlockSpec((1, d), lambda i, j: (0, 0)),
                  pl.BlockSpec((d, tn), lambda i, j: (0, j)),
                  pl.BlockSpec((d, 2 * LANES), lambda i, j: (0, 0))],
        out_specs=(pl.BlockSpec((tm, tn), lambda i, j: (i, j)),
                   pl.BlockSpec((tm, LANES), lambda i, j: (i, 0))),
        scratch_shapes=[pltpu.VMEM((tm, d), BF16)],
        compiler_params=_cparams(("parallel", "arbitrary")),
        name="in_proj",
    )(x2, shift, scale, norm_w, w_main, w_small)


def _bf16_round(x):
    u = int(np.float32(x).view(np.uint32))
    u = (u + 0x7FFF + ((u >> 16) & 1)) & 0xFFFF0000
    return float(np.uint32(u).view(np.float32))


LOG2E = 1.4426950408889634
LOG2E_HI = _bf16_round(LOG2E)
LOG2E_LO = _bf16_round(LOG2E - LOG2E_HI)
N_FEAT = 7


def _feature_lanes(lane, base, values):
    out = jnp.zeros(lane.shape, F32)
    for i, val in enumerate(values):
        out = jnp.where(lane == base + i, val, out)
    return out


def _attn_kernel(slope_ref, q_ref, k_ref, v_ref, lamv_ref, subw_ref, o_ref,
                 ke_ref, ve_ref, qe_ref, m_ref, acc_ref, *, tq, tk, seq, lam_init):
    h = pl.program_id(1)
    qi = pl.program_id(2)
    slope = slope_ref[h]
    hd = ATT_HEAD_DIM
    prep = 512 if seq % 512 == 0 else seq

    @pl.when(qi == 0)
    def _():
        def fill(i, carry):
            r0 = pl.multiple_of(i * prep, prep)
            kf = k_ref[0, pl.ds(r0, prep), :].astype(F32)
            lane = lax.broadcasted_iota(jnp.int32, (prep, LANES), 1)
            j = r0 + lax.broadcasted_iota(jnp.int32, (prep, LANES), 0)
            f_lo = (j & (LANES - 1)).astype(F32) * slope
            f_hi = (j >> (LANES.bit_length() - 1)).astype(F32) * (slope * LANES)
            one = jnp.ones((prep, LANES), F32)
            vals = (f_lo, f_lo, f_hi, f_hi, one, one, one)
            ke_ref[0, pl.ds(r0, prep), :] = jnp.where(lane < hd, kf, _feature_lanes(lane, hd, vals)).astype(BF16)
            ke_ref[1, pl.ds(r0, prep), :] = jnp.where(lane >= hd, kf, _feature_lanes(lane, 0, vals)).astype(BF16)
            ve_ref[pl.ds(r0, prep), 0:LANES] = v_ref[0, pl.ds(r0, prep), :]
            ve_ref[pl.ds(r0, prep), LANES:2 * LANES] = jnp.ones((prep, LANES), BF16)
            return carry

        lax.fori_loop(0, seq // prep, fill, 0)

    q_start = qi * tq
    lane = lax.broadcasted_iota(jnp.int32, (tq, LANES), 1)
    qf = q_ref[0].astype(F32) * (hd ** -0.5 * LOG2E)
    cq = jnp.full((tq, LANES), 1.0, F32) * (slope * (-LOG2E) * q_start.astype(F32))
    c_hi = cq.astype(BF16).astype(F32)
    c_mid = (cq - c_hi).astype(BF16).astype(F32)
    c_lo = cq - c_hi - c_mid
    qvals = (LOG2E_HI, LOG2E_LO, LOG2E_HI, LOG2E_LO, c_hi, c_mid, c_lo)
    qe_ref[0] = jnp.where(lane < hd, qf, _feature_lanes(lane, hd, qvals)).astype(BF16)
    qe_ref[1] = jnp.where(lane >= hd, qf, _feature_lanes(lane, 0, qvals)).astype(BF16)
    m_ref[...] = jnp.full(m_ref.shape, NEG_INF, F32)
    acc_ref[...] = jnp.zeros(acc_ref.shape, F32)

    n_full = q_start // tk

    def blocks(js, masked, r0=0, nr=tq):
        starts = [pl.multiple_of(j * tk, tk) for j in js]
        rows = slice(r0, r0 + nr)
        s_all = [[_dot_nt(qe_ref[mi, rows, :], ke_ref[mi, pl.ds(k0, tk), :]) for mi in range(2)] for k0 in starts]
        if any(masked):
            r = lax.broadcasted_iota(jnp.int32, (nr, tk), 0)
            c = lax.broadcasted_iota(jnp.int32, (nr, tk), 1)
            rel = c - r
        for mi in range(2):
            m = m_ref[mi, rows, :]
            acc = acc_ref[mi, rows, :]
            for bi, k0 in enumerate(starts):
                s = s_all[bi][mi]
                if masked[bi]:
                    s = jnp.where(rel <= (q_start + r0 - k0), s, NEG_INF)
                m_new = jnp.maximum(m, jnp.max(s, axis=1, keepdims=True))
                alpha = jnp.exp2(m - m_new)
                p = jnp.exp2(s - jnp.tile(m_new, (1, tk // LANES))).astype(BF16)
                acc = jnp.tile(alpha, (1, 2)) * acc + _dot(p, ve_ref[pl.ds(k0, tk), :])
                m = m_new
            m_ref[mi, rows, :] = m
            acc_ref[mi, rows, :] = acc

    odd = n_full & 1

    @pl.when(odd == 1)
    def _():
        blocks([0], [False])

    def loop_body(p, carry):
        blocks([odd + 2 * p, odd + 2 * p + 1], [False, False])
        return carry

    lax.fori_loop(0, n_full >> 1, loop_body, 0)
    n_diag = max(1, tq // tk)
    for g in range(n_diag):
        blocks([n_full + d for d in range(g + 1)], [False] * g + [True], g * (tq // n_diag), tq // n_diag)

    lamv = lamv_ref[...]
    lam = (jnp.exp(jnp.sum(lamv[0:1] * lamv[1:2], axis=1, keepdims=True))
           - jnp.exp(jnp.sum(lamv[2:3] * lamv[3:4], axis=1, keepdims=True)) + lam_init)
    a1 = acc_ref[0]
    a2 = acc_ref[1]
    o = a1[:, 0:LANES] / a1[:, LANES:2 * LANES] - lam * (a2[:, 0:LANES] / a2[:, LANES:2 * LANES])
    y = o * lax.rsqrt(jnp.mean(o * o, axis=-1, keepdims=True) + NORM_EPS)
    o_ref[0] = (y * subw_ref[...] * (1.0 - lam_init)).astype(BF16)


def _diff_attention(proj3, lamv, subln_w, lam_init):
    bsz, seq, _ = proj3.shape
    tq = min(ATT_TQ, seq)
    tk = min(ATT_TK, seq)
    slopes = jnp.exp2(-ALIBI_MAX_BIAS * jnp.arange(1, ATT_HEADS + 1, dtype=F32) / ATT_HEADS)
    kern = functools.partial(_attn_kernel, tq=tq, tk=tk, seq=seq, lam_init=lam_init)
    nq = C_QA // LANES
    nk = C_KA // LANES
    nv = C_VA // LANES
    return pl.pallas_call(
        kern,
        out_shape=jax.ShapeDtypeStruct((bsz, seq, ATT_HEADS * LANES), BF16),
        grid_spec=pltpu.PrefetchScalarGridSpec(
            num_scalar_prefetch=1,
            grid=(bsz, ATT_HEADS, seq // tq),
            in_specs=[pl.BlockSpec((1, tq, LANES), lambda b, h, i, s: (b, i, nq + h)),
                      pl.BlockSpec((1, seq, LANES), lambda b, h, i, s: (b, 0, nk + h)),
                      pl.BlockSpec((1, seq, LANES), lambda b, h, i, s: (b, 0, nv + h)),
                      pl.BlockSpec((4, LANES), lambda b, h, i, s: (0, 0)),
                      pl.BlockSpec((1, LANES), lambda b, h, i, s: (0, 0))],
            out_specs=pl.BlockSpec((1, tq, LANES), lambda b, h, i, s: (b, i, h)),
            scratch_shapes=[pltpu.VMEM((2, seq, LANES), BF16),
                            pltpu.VMEM((seq, 2 * LANES), BF16),
                            pltpu.VMEM((2, tq, LANES), BF16),
                            pltpu.VMEM((2, tq, LANES), F32),
                            pltpu.VMEM((2, tq, 2 * LANES), F32)]),
        compiler_params=_cparams(("parallel", "arbitrary", "arbitrary")),
        name="diff_attn",
    )(slopes, proj3, proj3, proj3, lamv, subln_w)


def _gdn_kernel(q_ref, k_ref, v_ref, z_ref, small_ref, cwq_ref, cwk_ref, cwv_ref,
                alog_ref, dtb_ref, nw_ref, o_ref, xbuf_ref, pre_ref, state_ref, mask_ref, *, heads, chunk):
    s_idx = pl.program_id(1)
    width = heads * LANES
    pad = 8

    n_levels = chunk.bit_length() - 1

    @pl.when(s_idx == 0)
    def _():
        xbuf_ref[:, 0:pad, :] = jnp.zeros((3, pad, width), F32)
        pre_ref[...] = jnp.zeros(pre_ref.shape, F32)
        state_ref[...] = jnp.zeros(state_ref.shape, F32)
        r = lax.broadcasted_iota(jnp.int32, (chunk, chunk), 0)
        c = lax.broadcasted_iota(jnp.int32, (chunk, chunk), 1)
        for k in range(n_levels):
            hit = ((r >> (k + 1)) == (c >> (k + 1))) & (((r >> k) & 1) == 1) & (((c >> k) & 1) == 0)
            mask_ref[k] = jnp.where(hit, 1.0, 0.0).astype(BF16)

    def conv_silu(idx, raw_ref, cw_ref, sl, anchor):
        xbuf_ref[idx, pad:pad + chunk, sl] = raw_ref[0, :, sl].astype(F32)
        cw = cw_ref[:, sl] + anchor
        acc = xbuf_ref[idx, pad:pad + chunk, sl] * cw[CONV_WIDTH - 1:CONV_WIDTH]
        for back in range(1, CONV_WIDTH):
            tap = CONV_WIDTH - 1 - back
            acc = acc + xbuf_ref[idx, pad - back:pad - back + chunk, sl] * cw[tap:tap + 1]
        xbuf_ref[idx, 0:pad, sl] = xbuf_ref[idx, chunk:chunk + pad, sl]
        return _silu(acc)

    small = small_ref[0]
    beta_all = _sigmoid(small)
    sp_in = small + dtb_ref[...]
    softplus = jnp.maximum(sp_in, 0.0) + jnp.log(1.0 + jnp.exp(-jnp.abs(sp_in)))
    g_all = -jnp.exp(alog_ref[...]) * softplus
    row = lax.broadcasted_iota(jnp.int32, (chunk, chunk), 0)
    col = lax.broadcasted_iota(jnp.int32, (chunk, chunk), 1)
    incl = col <= row
    tri = jnp.where(incl, 1.0, 0.0).astype(F32)
    gc_all = jnp.dot(tri, g_all, preferred_element_type=F32, precision=HIGHEST)
    eye = jnp.where(row == col, 1.0, 0.0).astype(F32)

    nw = nw_ref[...]
    hs = range(heads)
    sls = [slice(hh * LANES, (hh + 1) * LANES) for hh in hs]

    def prepare(hh, anchor):
        sl = sls[hh]
        qh = conv_silu(0, q_ref, cwq_ref, sl, anchor)
        kh = conv_silu(1, k_ref, cwk_ref, sl, anchor)
        pre_ref[0, :, sl] = qh * lax.rsqrt(jnp.sum(qh * qh, axis=-1, keepdims=True) + NORM_EPS) * (GDN_DK ** -0.5)
        pre_ref[1, :, sl] = kh * lax.rsqrt(jnp.sum(kh * kh, axis=-1, keepdims=True) + NORM_EPS)
        pre_ref[2, :, sl] = conv_silu(2, v_ref, cwv_ref, sl, anchor)

    todo = list(hs)

    def prepare_some(count, after):
        anchor = after[0:CONV_WIDTH, 0:LANES] * 0.0
        for _ in range(min(count, len(todo))):
            prepare(todo.pop(0), anchor)

    qn = [pre_ref[0, :, sls[hh]] for hh in hs]
    kn = [pre_ref[1, :, sls[hh]] for hh in hs]
    vc = [pre_ref[2, :, sls[hh]] for hh in hs]
    kb, gcs, egs, decay = [], [], [], []
    for hh in hs:
        kb.append(kn[hh] * beta_all[:, hh:hh + 1])
        gc = gc_all[:, heads + hh:heads + hh + 1]
        gcs.append(gc)
        egs.append(jnp.exp(gc))
        g_col = jnp.broadcast_to(gc, (chunk, chunk))
        decay.append(jnp.where(incl, jnp.exp(jnp.where(incl, g_col - g_col.T, 0.0)), 0.0))
    kn16 = [kn[hh].astype(BF16) for hh in hs]
    kk = [_dot_nt(kb[hh].astype(BF16), kn16[hh]) for hh in hs]
    qk = [_dot_nt(qn[hh].astype(BF16), kn16[hh]) for hh in hs]
    low = [(kk[hh] * decay[hh]).astype(BF16) for hh in hs]
    intra = [(qk[hh] * decay[hh]).astype(BF16) for hh in hs]
    inv = [eye - (low[hh] * mask_ref[0]).astype(F32) for hh in hs]
    for k in range(1, n_levels):
        inv16 = [inv[hh].astype(BF16) for hh in hs]
        t = [_dot(low[hh] * mask_ref[k], inv16[hh]).astype(BF16) for hh in hs]
        inv = [inv[hh] - _dot(inv16[hh], t[hh]) for hh in hs]
        prepare_some(1, inv[0])
    rhs = [jnp.concatenate([vc[hh] * beta_all[:, hh:hh + 1], kb[hh] * egs[hh]], axis=1).astype(BF16)
           for hh in hs]
    uw = [_dot(inv[hh].astype(BF16), rhs[hh]) for hh in hs]
    prepare_some(heads, uw[0])
    state = [state_ref[hh] for hh in hs]
    state16 = [state[hh].astype(BF16) for hh in hs]
    v_new = [(uw[hh][:, 0:LANES] - _dot(uw[hh][:, LANES:2 * LANES].astype(BF16), state16[hh])).astype(BF16)
             for hh in hs]
    for hh in hs:
        g_last = gcs[hh][chunk - 1:chunk, :]
        kdec = kn[hh] * jnp.exp(g_last - gcs[hh])
        state_ref[hh] = state[hh] * jnp.exp(g_last) + _dot(kdec.T.astype(BF16), v_new[hh])
    for hh in hs:
        o = _dot((qn[hh] * egs[hh]).astype(BF16), state16[hh]) + _dot(intra[hh], v_new[hh])
        y = o * lax.rsqrt(jnp.mean(o * o, axis=-1, keepdims=True) + NORM_EPS) * nw
        zf = z_ref[0, :, sls[hh]].astype(F32)
        o_ref[0, :, sls[hh]] = (y * _silu(zf)).astype(BF16)


def _gated_deltanet(proj3, small3, conv_w, a_log, dt_bias, norm_w):
    bsz, seq, _ = proj3.shape
    heads = GDN_HEADS
    width = heads * LANES
    chunk = min(GDN_CHUNK, seq)
    zeros = jnp.zeros((LANES - 2 * heads,), F32)
    alog_row = jnp.concatenate([jnp.zeros((heads,), F32), a_log, zeros]).reshape(1, LANES)
    dtb_row = jnp.concatenate([jnp.zeros((heads,), F32), dt_bias, zeros]).reshape(1, LANES)
    kern = functools.partial(_gdn_kernel, heads=heads, chunk=chunk)
    n_chunks = seq // chunk
    nxt = lambda s: jnp.minimum(s, n_chunks - 1)
    cur = lambda s: jnp.maximum(s - 1, 0)
    raw = lambda off: pl.BlockSpec((1, chunk, width), lambda b, s: (b, nxt(s), off // width))
    cw = lambda j: pl.BlockSpec((CONV_WIDTH, width), lambda b, s: (0, j))
    row = pl.BlockSpec((1, LANES), lambda b, s: (0, 0))
    return pl.pallas_call(
        kern,
        out_shape=jax.ShapeDtypeStruct((bsz, seq, width), BF16),
        grid=(bsz, n_chunks + 1),
        in_specs=[raw(C_QB), raw(C_KB), raw(C_VB),
                  pl.BlockSpec((1, chunk, width), lambda b, s: (b, cur(s), C_ZB // width)),
                  pl.BlockSpec((1, chunk, LANES), lambda b, s: (b, cur(s), 0)),
                  cw(0), cw(1), cw(2), row, row, row],
        out_specs=pl.BlockSpec((1, chunk, width), lambda b, s: (b, cur(s), 0)),
        scratch_shapes=[pltpu.VMEM((3, chunk + 8, width), F32),
                        pltpu.VMEM((3, chunk, width), F32),
                        pltpu.VMEM((heads, GDN_DK, LANES), F32),
                        pltpu.VMEM((chunk.bit_length() - 1, chunk, chunk), BF16)],
        compiler_params=_cparams(("parallel", "arbitrary")),
        name="gated_deltanet",
    )(proj3, proj3, proj3, proj3, small3, conv_w, conv_w, conv_w, alog_row, dtb_row, norm_w)


def _pack_bf16_pairs(x):
    k = x.shape[1] // 2
    r = pltpu.bitcast(x.astype(BF16).astype(F32), jnp.uint32)
    return (r[:, k:] & jnp.uint32(0xFFFF0000)) | (r[:, :k] >> 16)


def _unpack_bf16_pairs(p):
    lo = pltpu.bitcast(p << 16, F32)
    hi = pltpu.bitcast(p & jnp.uint32(0xFFFF0000), F32)
    return jnp.concatenate([lo, hi], axis=1).astype(BF16)


def _merge_kernel(ya_ref, yb_ref, ga_ref, gb_ref, x_ref, gate1_ref, nw_ref, shift_ref, scale_ref,
                  wa_ref, wb_ref, wo_ref, wr_ref, xo_ref, h2_ref, lg_ref):
    tm = x_ref.shape[0]
    sub = min(MERGE_SUB, tm)
    rows = [slice(i * sub, (i + 1) * sub) for i in range(tm // sub)]
    a = [_dot(ya_ref[r, :], wa_ref[...]) for r in rows]
    b = [_dot(yb_ref[r, :], wb_ref[...]) for r in rows]
    mixed = [(_sigmoid(ga_ref[r, :].astype(F32)) * a[i] + _sigmoid(gb_ref[r, :].astype(F32)) * b[i]).astype(BF16)
             for i, r in enumerate(rows)]
    upd = [_dot(m, wo_ref[...]) for m in mixed]
    for i, r in enumerate(rows):
        xn = x_ref[r, :] + gate1_ref[0] * upd[i]
        xo_ref[r, :] = xn
        ms = jnp.mean(xn * xn, axis=-1, keepdims=True)
        h2 = xn * lax.rsqrt(ms + NORM_EPS) * nw_ref[...]
        h2 = h2 * (1.0 + scale_ref[0]) + shift_ref[0]
        h2_ref[r, :] = _pack_bf16_pairs(h2)
        lg_ref[r, :] = _dot_split(h2, wr_ref)


def _merge_out(ya, yb, proj, x2, gate1, norm_w, shift, scale, wa, wb, wo, wr, seq):
    n, d = x2.shape
    tm = min(512, seq)
    per_b = seq // tm
    g0 = C_GATES // d
    tok = lambda j: pl.BlockSpec((tm, d), lambda i: (i, j))
    per_batch = pl.BlockSpec((1, 1, d), lambda i: (i // per_b, 0, 0))
    full = lambda r, c: pl.BlockSpec((r, c), lambda i: (0, 0))
    return pl.pallas_call(
        _merge_kernel,
        out_shape=(jax.ShapeDtypeStruct((n, d), F32),
                   jax.ShapeDtypeStruct((n, d // 2), jnp.uint32),
                   jax.ShapeDtypeStruct((n, LANES), F32)),
        grid=(n // tm,),
        in_specs=[tok(0), tok(0), tok(g0), tok(g0 + 1), tok(0), per_batch, full(1, d), per_batch, per_batch,
                  full(d, d), full(d, d), full(d, d), full(d, 2 * LANES)],
        out_specs=(tok(0), pl.BlockSpec((tm, d // 2), lambda i: (i, 0)),
                   pl.BlockSpec((tm, LANES), lambda i: (i, 0))),
        compiler_params=_cparams(("parallel",)),
        name="merge_out",
    )(ya, yb, proj, proj, x2, gate1, norm_w, shift, scale, wa, wb, wo, wr)


def _dispatch_kernel(d0_ref, d1_ref, h2_ref, xs_in_ref, xs_ref, sem, *, tile):
    del xs_in_ref
    d_refs = (d0_ref, d1_ref)

    def row_copy(t, k):
        return pltpu.make_async_copy(h2_ref.at[pl.ds(t, 1), :],
                                     xs_ref.at[pl.ds(d_refs[k][t], 1), :], sem)

    def start(t, carry):
        for k in range(TOP_K):
            row_copy(t, k).start()
        return carry

    def wait(t, carry):
        for k in range(TOP_K):
            row_copy(t, k).wait()
        return carry

    lax.fori_loop(0, tile, start, 0, unroll=DMA_UNROLL)
    lax.fori_loop(0, tile, wait, 0, unroll=DMA_UNROLL)


def _dispatch(dests, h2p, n_rows):
    n, half = h2p.shape
    tile = min(DISPATCH_TILE, n)
    xs0 = jnp.zeros((n_rows, half), jnp.uint32)
    kern = functools.partial(_dispatch_kernel, tile=tile)
    idx = pl.BlockSpec((tile,), lambda i: (i,), memory_space=pltpu.SMEM)
    return pl.pallas_call(
        kern,
        out_shape=jax.ShapeDtypeStruct((n_rows, half), jnp.uint32),
        grid=(n // tile,),
        in_specs=[idx, idx,
                  pl.BlockSpec((tile, half), lambda i: (i, 0)),
                  pl.BlockSpec(memory_space=pl.ANY)],
        out_specs=pl.BlockSpec(memory_space=pl.ANY),
        scratch_shapes=[pltpu.SemaphoreType.DMA],
        input_output_aliases={3: 0},
        compiler_params=_cparams(("arbitrary",)),
        name="moe_dispatch",
    )(dests[0], dests[1], h2p, xs0)


def _expert_kernel(be_ref, used_ref, x_ref, w1_ref, w3_ref, w2_ref, y_ref, w1b_ref, w3b_ref, w2b_ref):
    i = pl.program_id(0)
    live = i < used_ref[0]
    new_expert = jnp.logical_or(i == 0, be_ref[i] != be_ref[jnp.maximum(i - 1, 0)])

    @pl.when(new_expert)
    def _():
        w1b_ref[...] = w1_ref[0, 0].astype(BF16)
        w3b_ref[...] = w3_ref[0, 0].astype(BF16)
        w2b_ref[...] = w2_ref[0, 0].astype(BF16)

    @pl.when(live)
    def _():
        x = _unpack_bf16_pairs(x_ref[...])
        a = _dot(x, w1b_ref[...])
        g = _dot(x, w3b_ref[...])
        y_ref[...] = _dot((_silu(a) * g).astype(BF16), w2b_ref[...])

    @pl.when(jnp.logical_not(live))
    def _():
        y_ref[...] = jnp.zeros(y_ref.shape, F32)


def _experts(block_e, used, xs, w1, w3, w2, layer):
    n_rows, half = xs.shape
    d = 2 * half
    n_blocks = n_rows // MOE_ROWS
    live = lambda i, be, used: jnp.minimum(i, used[0] - 1)
    wspec = lambda r, c: pl.BlockSpec((1, 1, r, c), lambda i, be, used: (layer, be[i], 0, 0))
    return pl.pallas_call(
        _expert_kernel,
        out_shape=jax.ShapeDtypeStruct((n_rows, d), F32),
        grid_spec=pltpu.PrefetchScalarGridSpec(
            num_scalar_prefetch=2,
            grid=(n_blocks,),
            in_specs=[pl.BlockSpec((MOE_ROWS, half), lambda i, be, used: (live(i, be, used), 0)),
                      wspec(d, D_EXPERT), wspec(d, D_EXPERT), wspec(D_EXPERT, d)],
            out_specs=pl.BlockSpec((MOE_ROWS, d), lambda i, be, used: (i, 0)),
            scratch_shapes=[pltpu.VMEM((d, D_EXPERT), BF16), pltpu.VMEM((d, D_EXPERT), BF16),
                            pltpu.VMEM((D_EXPERT, d), BF16)]),
        compiler_params=_cparams(("arbitrary",)),
        name="moe_experts",
    )(block_e, used, xs, w1, w3, w2)


def _combine_kernel(c0_ref, c1_ref, n0_ref, n1_ref, ys_ref, x_ref, wt_ref, gate2_ref, fw_ref, o_ref, buf_ref,
                    sems, *, tile, final):
    i = pl.program_id(0)
    slot = i & 1
    cur = (c0_ref, c1_ref)
    nxt = (n0_ref, n1_ref)

    def row_copy(d_refs, t, k, s):
        return pltpu.make_async_copy(ys_ref.at[pl.ds(d_refs[k][t], 1), :],
                                     buf_ref.at[s, k, pl.ds(t, 1), :], sems.at[s])

    def gather(d_refs, s):
        def start(t, carry):
            for k in range(TOP_K):
                row_copy(d_refs, t, k, s).start()
            return carry

        lax.fori_loop(0, tile, start, 0, unroll=DMA_UNROLL)

    @pl.when(i == 0)
    def _():
        gather(cur, 0)

    @pl.when(i < pl.num_programs(0) - 1)
    def _():
        gather(nxt, 1 - slot)

    def wait(t, carry):
        for k in range(TOP_K):
            row_copy(cur, t, k, slot).wait()
        return carry

    lax.fori_loop(0, tile, wait, 0, unroll=DMA_UNROLL)
    wt = wt_ref[...]
    y = wt[:, 0:1] * buf_ref[slot, 0] + wt[:, 1:2] * buf_ref[slot, 1]
    xn = x_ref[...] + gate2_ref[0] * y
    if final:
        xn = xn * lax.rsqrt(jnp.mean(xn * xn, axis=-1, keepdims=True) + NORM_EPS) * fw_ref[...]
    o_ref[...] = xn


def _combine(dests, ys, x2, wts, gate2, final_w, seq, final):
    n, d = x2.shape
    tile = min(TOK_TILE, seq)
    per_b = seq // tile
    steps = n // tile
    kern = functools.partial(_combine_kernel, tile=tile, final=final)
    cur = pl.BlockSpec((tile,), lambda i: (i,), memory_space=pltpu.SMEM)
    nxt = pl.BlockSpec((tile,), lambda i: (jnp.minimum(i + 1, steps - 1),), memory_space=pltpu.SMEM)
    return pl.pallas_call(
        kern,
        out_shape=jax.ShapeDtypeStruct((n, d), F32),
        grid=(steps,),
        in_specs=[cur, cur, nxt, nxt,
                  pl.BlockSpec(memory_space=pl.ANY),
                  pl.BlockSpec((tile, d), lambda i: (i, 0)),
                  pl.BlockSpec((tile, LANES), lambda i: (i, 0)),
                  pl.BlockSpec((1, 1, d), lambda i: (i // per_b, 0, 0)),
                  pl.BlockSpec((1, d), lambda i: (0, 0))],
        out_specs=pl.BlockSpec((tile, d), lambda i: (i, 0)),
        scratch_shapes=[pltpu.VMEM((2, TOP_K, tile, d), F32), pltpu.SemaphoreType.DMA((2,))],
        compiler_params=_cparams(("arbitrary",)),
        name="moe_combine",
    )(dests[0], dests[1], dests[0], dests[1], ys, x2, wts, gate2, final_w)


def _route_kernel(lg_ref, bias_ref, dest_ref, wt_ref, cnt_ref, tri_ref, base_ref, start_ref, *, tl):
    phase = pl.program_id(0)
    i = pl.program_id(1)
    ne = N_EXPERTS
    eg = EXPERTS_PER_GROUP

    @pl.when(jnp.logical_and(phase == 0, i == 0))
    def _():
        base_ref[...] = jnp.zeros(base_ref.shape, F32)
        r = lax.broadcasted_iota(jnp.int32, (tl, tl), 0)
        c = lax.broadcasted_iota(jnp.int32, (tl, tl), 1)
        tri_ref[...] = jnp.where(r < c, 1.0, 0.0).astype(BF16)

    lt = lg_ref[...].T + bias_ref[...]
    gl = lt[0:N_GROUPS]
    gmax = jnp.max(gl, axis=0, keepdims=True)
    gsum = jnp.sum(jnp.exp(gl - gmax), axis=0, keepdims=True)
    grow = lax.broadcasted_iota(jnp.int32, (N_GROUPS, tl), 0)
    gsel = jnp.min(jnp.where(gl == gmax, grow, N_GROUPS), axis=0, keepdims=True)
    gweight = 1.0 / gsum
    el = lt[8:8 + ne]
    sel = el[0:eg]
    for g in range(1, N_GROUPS):
        sel = jnp.where(gsel == g, el[g * eg:(g + 1) * eg], sel)
    erow = lax.broadcasted_iota(jnp.int32, (eg, tl), 0)
    v1 = jnp.max(sel, axis=0, keepdims=True)
    i1 = jnp.min(jnp.where(sel == v1, erow, eg), axis=0, keepdims=True)
    rest = jnp.where(erow == i1, NEG_INF, sel)
    v2 = jnp.max(rest, axis=0, keepdims=True)
    i2 = jnp.min(jnp.where(rest == v2, erow, eg), axis=0, keepdims=True)
    esum = jnp.sum(jnp.exp(sel - v1), axis=0, keepdims=True)
    p1 = 1.0 / esum
    p2 = jnp.exp(v2 - v1) / esum
    w1 = gweight * p1 / (p1 + p2)
    w2 = gweight * p2 / (p1 + p2)
    eio = lax.broadcasted_iota(jnp.int32, (ne, tl), 0)
    oh1 = jnp.where(eio == gsel * eg + i1, 1.0, 0.0)
    oh2 = jnp.where(eio == gsel * eg + i2, 1.0, 0.0)
    oh = oh1 + oh2
    tile_cnt = jnp.sum(oh, axis=1, keepdims=True)

    @pl.when(phase == 0)
    def _():
        base_ref[...] = base_ref[...] + tile_cnt

    @pl.when(jnp.logical_and(phase == 1, i == 0))
    def _():
        cnt = base_ref[...]
        cnt_ref[...] = cnt
        padded = jnp.floor((cnt + (MOE_ROWS - 1)) * (1.0 / MOE_ROWS)) * MOE_ROWS
        r = lax.broadcasted_iota(jnp.int32, (ne, ne), 0)
        c = lax.broadcasted_iota(jnp.int32, (ne, ne), 1)
        below = jnp.where(c < r, 1.0, 0.0).astype(F32)
        start_ref[...] = jnp.dot(below, padded, preferred_element_type=F32, precision=HIGHEST)
        base_ref[...] = jnp.zeros(base_ref.shape, F32)

    @pl.when(phase == 1)
    def _():
        earlier = _dot(oh.astype(BF16), tri_ref[...]) + base_ref[:, 0:1]
        pos = earlier + start_ref[:, 0:1]
        d1 = jnp.sum(oh1 * pos, axis=0, keepdims=True)
        d2 = jnp.sum(oh2 * pos, axis=0, keepdims=True)
        row8 = lax.broadcasted_iota(jnp.int32, (8, tl), 0)
        dest_ref[...] = jnp.where(row8 == 0, d1, jnp.where(row8 == 1, d2, 0.0)).astype(jnp.int32)
        rowl = lax.broadcasted_iota(jnp.int32, (LANES, tl), 0)
        wt_ref[...] = jnp.where(rowl == 0, w1, jnp.where(rowl == 1, w2, 0.0)).T
        base_ref[...] = base_ref[...] + tile_cnt


def _route(logits, b_rg, b_re):
    n = logits.shape[0]
    tl = min(ROUTE_TILE, n)
    bias = jnp.zeros((LANES,), F32).at[0:N_GROUPS].set(b_rg).at[8:8 + N_EXPERTS].set(b_re).reshape(LANES, 1)
    kern = functools.partial(_route_kernel, tl=tl)
    dest, wts, cnt = pl.pallas_call(
        kern,
        out_shape=(jax.ShapeDtypeStruct((8, n), jnp.int32), jax.ShapeDtypeStruct((n, LANES), F32),
                   jax.ShapeDtypeStruct((N_EXPERTS, LANES), F32)),
        grid=(2, n // tl),
        in_specs=[pl.BlockSpec((tl, LANES), lambda p, i: (i, 0)),
                  pl.BlockSpec((LANES, 1), lambda p, i: (0, 0))],
        out_specs=(pl.BlockSpec((8, tl), lambda p, i: (0, i * p)),
                   pl.BlockSpec((tl, LANES), lambda p, i: (i * p, 0)),
                   pl.BlockSpec((N_EXPERTS, LANES), lambda p, i: (0, 0))),
        scratch_shapes=[pltpu.VMEM((tl, tl), BF16), pltpu.VMEM((N_EXPERTS, LANES), F32),
                        pltpu.VMEM((N_EXPERTS, LANES), F32)],
        compiler_params=_cparams(("arbitrary", "arbitrary")),
        name="moe_route",
    )(logits, bias)
    counts = cnt[:, 0].astype(jnp.int32)
    padded = (counts + MOE_ROWS - 1) // MOE_ROWS * MOE_ROWS
    pad_end = jnp.cumsum(padded)
    n_blocks = (n * TOP_K) // MOE_ROWS + N_EXPERTS
    used = jnp.maximum(pad_end[-1] // MOE_ROWS, 1).astype(jnp.int32)
    blk = jnp.minimum(jnp.arange(n_blocks, dtype=jnp.int32), used - 1)
    block_e = jnp.sum((pad_end[None, :] <= (blk * MOE_ROWS)[:, None]).astype(jnp.int32), axis=1)
    block_e = jnp.minimum(block_e, N_EXPERTS - 1).astype(jnp.int32)
    return (dest[0], dest[1]), wts, block_e, used.reshape(1), n_blocks * MOE_ROWS


def kernel(x, c, ada_w, ada_b, norm1_w, w_in, conv_w, lambda_q1, lambda_k1, lambda_q2, lambda_k2,
           subln_w, a_log, dt_bias, gdn_norm_w, w_branch_a, w_branch_b, w_out, norm2_w,
           router_group_w, router_group_b, router_expert_w, router_expert_b,
           expert_w1, expert_w3, expert_w2, final_norm_w):
    bsz, seq, d = x.shape
    depth = ada_w.shape[0]
    n = bsz * seq
    mod = _ada_mod(c, ada_w, ada_b)
    x2 = x.reshape(n, d)
    n_beta = 7168
    for l in range(depth):
        lam_init = 0.8 - 0.6 * math.exp(-0.3 * l)
        m = mod[l].reshape(bsz, 1, 6, d)
        shift1, scale1, gate1, shift2, scale2, gate2 = (m[:, :, i] for i in range(6))
        w = w_in[l]
        w_main = jnp.concatenate([w[:, :n_beta], w[:, n_beta + 2 * GDN_HEADS:]], axis=1).astype(BF16)
        w_small = _split_hi_lo(jnp.pad(w[:, n_beta:n_beta + 2 * GDN_HEADS], ((0, 0), (0, LANES - 2 * GDN_HEADS))))
        proj, small = _in_proj(x2, shift1, scale1, norm1_w[l].reshape(1, d), w_main, w_small, seq)
        proj3 = proj.reshape(bsz, seq, P_MAIN)
        lamv = jnp.pad(jnp.stack([lambda_q1[l], lambda_k1[l], lambda_q2[l], lambda_k2[l]]),
                       ((0, 0), (0, LANES - ATT_HEAD_DIM)))
        ya = _diff_attention(proj3, lamv, subln_w[l].reshape(1, LANES), lam_init)
        yb = _gated_deltanet(proj3, small.reshape(bsz, seq, LANES), conv_w[l], a_log[l], dt_bias[l],
                             gdn_norm_w[l].reshape(1, LANES))
        wr = jnp.zeros((d, LANES), F32)
        wr = wr.at[:, 0:N_GROUPS].set(router_group_w[l]).at[:, 8:8 + N_EXPERTS].set(router_expert_w[l])
        x2, h2p, logits = _merge_out(
            ya.reshape(n, d), yb.reshape(n, d), proj, x2, gate1, norm2_w[l].reshape(1, d), shift2, scale2,
            w_branch_a[l].astype(BF16), w_branch_b[l].astype(BF16), w_out[l].astype(BF16), _split_hi_lo(wr), seq)
        dest, wts, block_e, used, n_rows = _route(logits, router_group_b[l], router_expert_b[l])
        xs = _dispatch(dest, h2p, n_rows)
        ys = _experts(block_e, used, xs, expert_w1, expert_w3, expert_w2, l)
        x2 = _combine(dest, ys, x2, wts, gate2, final_norm_w.reshape(1, d), seq, final=(l == depth - 1))
    return x2.reshape(bsz, seq, d)
```

```python
import functools
import math

import jax
import jax.numpy as jnp
import numpy as np
from jax import lax
from jax.experimental import pallas as pl
from jax.experimental.pallas import tpu as pltpu

F32 = jnp.float32
BF16 = jnp.bfloat16
HIGHEST = lax.Precision.HIGHEST

D_MODEL = 1024
LANES = 128
ATT_HEADS = 8
ATT_HEAD_DIM = 64
ALIBI_MAX_BIAS = 8.0
GDN_HEADS = 8
GDN_DK = 128
CONV_WIDTH = 4
GDN_CHUNK = 128
N_GROUPS = 4
EXPERTS_PER_GROUP = 8
N_EXPERTS = N_GROUPS * EXPERTS_PER_GROUP
TOP_K = 2
D_EXPERT = D_MODEL // 2
NORM_EPS = 1e-6
NEG_INF = -1e30

C_QA, C_KA, C_VA = 0, 1024, 2048
C_QB, C_KB, C_VB = 3072, 4096, 5120
C_ZB = 6144
C_GATES = 7168
P_MAIN = 9216

ATT_TQ = 2048
ATT_TK = 512
MERGE_SUB = 256
MOE_ROWS = 512
TOK_TILE = 512
DISPATCH_TILE = 512
DMA_UNROLL = 8
ROUTE_TILE = 2048
VMEM_LIMIT = 56 * 1024 * 1024


def _cparams(sem):
    return pltpu.CompilerParams(dimension_semantics=sem, vmem_limit_bytes=VMEM_LIMIT)


def _sigmoid(x):
    return 1.0 / (1.0 + jnp.exp(-x))


def _silu(x):
    return x * _sigmoid(x)


def _dot(a, b):
    return jnp.dot(a, b, preferred_element_type=F32)


def _split_hi_lo(w):
    hi = w.astype(BF16)
    lo = (w - hi.astype(F32)).astype(BF16)
    return jnp.concatenate([hi, lo], axis=1)


def _dot_split(x, w2_ref):
    hi = x.astype(BF16)
    lo = (x - hi.astype(F32)).astype(BF16)
    r = _dot(hi, w2_ref[...])
    return r[:, 0:LANES] + r[:, LANES:2 * LANES] + _dot(lo, w2_ref[:, 0:LANES])


def _dot_nt(a, b):
    return lax.dot_general(a, b, (((1,), (1,)), ((), ())), preferred_element_type=F32)


def _ada_kernel(c_ref, w_ref, b_ref, o_ref):
    cond = _silu(c_ref[...])
    o_ref[0] = jnp.dot(cond, w_ref[0], preferred_element_type=F32, precision=HIGHEST) + b_ref[0]


def _ada_mod(c, ada_w, ada_b):
    depth, d, d6 = ada_w.shape
    bsz = c.shape[0]
    tn = 1536
    return pl.pallas_call(
        _ada_kernel,
        out_shape=jax.ShapeDtypeStruct((depth, bsz, d6), F32),
        grid=(depth, d6 // tn),
        in_specs=[pl.BlockSpec((bsz, d), lambda l, j: (0, 0)),
                  pl.BlockSpec((1, d, tn), lambda l, j: (l, 0, j)),
                  pl.BlockSpec((1, 1, tn), lambda l, j: (l, 0, j))],
        out_specs=pl.BlockSpec((1, bsz, tn), lambda l, j: (l, 0, j)),
        compiler_params=_cparams(("arbitrary", "arbitrary")),
        name="ada_mod",
    )(c, ada_w, ada_b.reshape(depth, 1, d6))


def _inproj_kernel(x_ref, shift_ref, scale_ref, nw_ref, w_ref, ws_ref, o_ref, os_ref, hn_ref):
    @pl.when(pl.program_id(1) == 0)
    def _():
        x = x_ref[...]
        ms = jnp.mean(x * x, axis=-1, keepdims=True)
        hn = x * lax.rsqrt(ms + NORM_EPS) * nw_ref[...]
        hn = hn * (1.0 + scale_ref[0]) + shift_ref[0]
        hn_ref[...] = hn.astype(BF16)
        os_ref[...] = _dot_split(hn, ws_ref)

    o_ref[...] = _dot(hn_ref[...], w_ref[...]).astype(BF16)


def _in_proj(x2, shift, scale, norm_w, w_main, w_small, seq):
    n, d = x2.shape
    tm = min(1024, seq)
    tn = 4608
    per_b = seq // tm
    return pl.pallas_call(
        _inproj_kernel,
        out_shape=(jax.ShapeDtypeStruct((n, P_MAIN), BF16), jax.ShapeDtypeStruct((n, LANES), F32)),
        grid=(n // tm, P_MAIN // tn),
        in_specs=[pl.BlockSpec((tm, d), lambda i, j: (i, 0)),
                  pl.BlockSpec((1, 1, d), lambda i, j: (i // per_b, 0, 0)),
                  pl.BlockSpec((1, 1, d), lambda i, j: (i // per_b, 0, 0)),
                  pl.BlockSpec((1, d), lambda i, j: (0, 0)),
                  pl.BlockSpec((d, tn), lambda i, j: (0, j)),
                  pl.BlockSpec((d, 2 * LANES), lambda i, j: (0, 0))],
        out_specs=(pl.BlockSpec((tm, tn), lambda i, j: (i, j)),
                   pl.BlockSpec((tm, LANES), lambda i, j: (i, 0))),
        scratch_shapes=[pltpu.VMEM((tm, d), BF16)],
        compiler_params=_cparams(("parallel", "arbitrary")),
        name="in_proj",
    )(x2, shift, scale, norm_w, w_main, w_small)


def _bf16_round(x):
    u = int(np.float32(x).view(np.uint32))
    u = (u + 0x7FFF + ((u >> 16) & 1)) & 0xFFFF0000
    return float(np.uint32(u).view(np.float32))


LOG2E = 1.4426950408889634
LOG2E_HI = _bf16_round(LOG2E)
LOG2E_LO = _bf16_round(LOG2E - LOG2E_HI)


def _feature_lanes(lane, base, values):
    out = jnp.zeros(lane.shape, F32)
    for i, val in enumerate(values):
        out = jnp.where(lane == base + i, val, out)
    return out


def _attn_kernel(slope_ref, q_ref, k_ref, v_ref, lamv_ref, subw_ref, o_ref,
                 ke_ref, ve_ref, qe_ref, m_ref, acc_ref, *, tq, tk, seq, lam_init):
    h = pl.program_id(1)
    qi = pl.program_id(2)
    slope = slope_ref[h]
    hd = ATT_HEAD_DIM
    prep = 512 if seq % 512 == 0 else seq

    @pl.when(qi == 0)
    def _():
        def fill(i, carry):
            r0 = pl.multiple_of(i * prep, prep)
            kf = k_ref[0, pl.ds(r0, prep), :].astype(F32)
            lane = lax.broadcasted_iota(jnp.int32, (prep, LANES), 1)
            j = r0 + lax.broadcasted_iota(jnp.int32, (prep, LANES), 0)
            f_lo = (j & (LANES - 1)).astype(F32) * slope
            f_hi = (j >> (LANES.bit_length() - 1)).astype(F32) * (slope * LANES)
            one = jnp.ones((prep, LANES), F32)
            vals = (f_lo, f_lo, f_hi, f_hi, one, one, one)
            ke_ref[0, pl.ds(r0, prep), :] = jnp.where(lane < hd, kf, _feature_lanes(lane, hd, vals)).astype(BF16)
            ke_ref[1, pl.ds(r0, prep), :] = jnp.where(lane >= hd, kf, _feature_lanes(lane, 0, vals)).astype(BF16)
            ve_ref[pl.ds(r0, prep), 0:LANES] = v_ref[0, pl.ds(r0, prep), :]
            ve_ref[pl.ds(r0, prep), LANES:2 * LANES] = jnp.ones((prep, LANES), BF16)
            return carry

        lax.fori_loop(0, seq // prep, fill, 0)

    q_start = qi * tq
    lane = lax.broadcasted_iota(jnp.int32, (tq, LANES), 1)
    qf = q_ref[0].astype(F32) * (hd ** -0.5 * LOG2E)
    cq = jnp.full((tq, LANES), 1.0, F32) * (slope * (-LOG2E) * q_start.astype(F32))
    c_hi = cq.astype(BF16).astype(F32)
    c_mid = (cq - c_hi).astype(BF16).astype(F32)
    c_lo = cq - c_hi - c_mid
    qvals = (LOG2E_HI, LOG2E_LO, LOG2E_HI, LOG2E_LO, c_hi, c_mid, c_lo)
    qe_ref[0] = jnp.where(lane < hd, qf, _feature_lanes(lane, hd, qvals)).astype(BF16)
    qe_ref[1] = jnp.where(lane >= hd, qf, _feature_lanes(lane, 0, qvals)).astype(BF16)
    m_ref[...] = jnp.full(m_ref.shape, NEG_INF, F32)
    acc_ref[...] = jnp.zeros(acc_ref.shape, F32)

    n_full = q_start // tk

    def blocks(js, masked, r0=0, nr=tq):
        starts = [pl.multiple_of(j * tk, tk) for j in js]
        rows = slice(r0, r0 + nr)
        s_all = [[_dot_nt(qe_ref[mi, rows, :], ke_ref[mi, pl.ds(k0, tk), :]) for mi in range(2)] for k0 in starts]
        if any(masked):
            r = lax.broadcasted_iota(jnp.int32, (nr, tk), 0)
            c = lax.broadcasted_iota(jnp.int32, (nr, tk), 1)
            rel = c - r
        for mi in range(2):
            m = m_ref[mi, rows, :]
            acc = acc_ref[mi, rows, :]
            for bi, k0 in enumerate(starts):
                s = s_all[bi][mi]
                if masked[bi]:
                    s = jnp.where(rel <= (q_start + r0 - k0), s, NEG_INF)
                m_new = jnp.maximum(m, jnp.max(s, axis=1, keepdims=True))
                alpha = jnp.exp2(m - m_new)
                p = jnp.exp2(s - jnp.tile(m_new, (1, tk // LANES))).astype(BF16)
                acc = jnp.tile(alpha, (1, 2)) * acc + _dot(p, ve_ref[pl.ds(k0, tk), :])
                m = m_new
            m_ref[mi, rows, :] = m
            acc_ref[mi, rows, :] = acc

    odd = n_full & 1

    @pl.when(odd == 1)
    def _():
        blocks([0], [False])

    def loop_body(p, carry):
        blocks([odd + 2 * p, odd + 2 * p + 1], [False, False])
        return carry

    lax.fori_loop(0, n_full >> 1, loop_body, 0)
    n_diag = max(1, tq // tk)
    for g in range(n_diag):
        blocks([n_full + d for d in range(g + 1)], [False] * g + [True], g * (tq // n_diag), tq // n_diag)

    lamv = lamv_ref[...]
    lam = (jnp.exp(jnp.sum(lamv[0:1] * lamv[1:2], axis=1, keepdims=True))
           - jnp.exp(jnp.sum(lamv[2:3] * lamv[3:4], axis=1, keepdims=True)) + lam_init)
    a1 = acc_ref[0]
    a2 = acc_ref[1]
    o = a1[:, 0:LANES] / a1[:, LANES:2 * LANES] - lam * (a2[:, 0:LANES] / a2[:, LANES:2 * LANES])
    y = o * lax.rsqrt(jnp.mean(o * o, axis=-1, keepdims=True) + NORM_EPS)
    o_ref[0] = (y * subw_ref[...] * (1.0 - lam_init)).astype(BF16)


def _diff_attention(proj3, lamv, subln_w, lam_init):
    bsz, seq, _ = proj3.shape
    tq = min(ATT_TQ, seq)
    tk = min(ATT_TK, seq)
    slopes = jnp.exp2(-ALIBI_MAX_BIAS * jnp.arange(1, ATT_HEADS + 1, dtype=F32) / ATT_HEADS)
    kern = functools.partial(_attn_kernel, tq=tq, tk=tk, seq=seq, lam_init=lam_init)
    nq = C_QA // LANES
    nk = C_KA // LANES
    nv = C_VA // LANES
    return pl.pallas_call(
        kern,
        out_shape=jax.ShapeDtypeStruct((bsz, seq, ATT_HEADS * LANES), BF16),
        grid_spec=pltpu.PrefetchScalarGridSpec(
            num_scalar_prefetch=1,
            grid=(bsz, ATT_HEADS, seq // tq),
            in_specs=[pl.BlockSpec((1, tq, LANES), lambda b, h, i, s: (b, i, nq + h)),
                      pl.BlockSpec((1, seq, LANES), lambda b, h, i, s: (b, 0, nk + h)),
                      pl.BlockSpec((1, seq, LANES), lambda b, h, i, s: (b, 0, nv + h)),
                      pl.BlockSpec((4, LANES), lambda b, h, i, s: (0, 0)),
                      pl.BlockSpec((1, LANES), lambda b, h, i, s: (0, 0))],
            out_specs=pl.BlockSpec((1, tq, LANES), lambda b, h, i, s: (b, i, h)),
            scratch_shapes=[pltpu.VMEM((2, seq, LANES), BF16),
                            pltpu.VMEM((seq, 2 * LANES), BF16),
                            pltpu.VMEM((2, tq, LANES), BF16),
                            pltpu.VMEM((2, tq, LANES), F32),
                            pltpu.VMEM((2, tq, 2 * LANES), F32)]),
        compiler_params=_cparams(("parallel", "arbitrary", "arbitrary")),
        name="diff_attn",
    )(slopes, proj3, proj3, proj3, lamv, subln_w)


def _gdn_kernel(q_ref, k_ref, v_ref, z_ref, small_ref, cwq_ref, cwk_ref, cwv_ref,
                alog_ref, dtb_ref, nw_ref, o_ref, xbuf_ref, pre_ref, state_ref, mask_ref, *, heads, chunk):
    s_idx = pl.program_id(1)
    width = heads * LANES
    pad = 8

    n_levels = chunk.bit_length() - 1

    @pl.when(s_idx == 0)
    def _():
        xbuf_ref[:, 0:pad, :] = jnp.zeros((3, pad, width), F32)
        pre_ref[...] = jnp.zeros(pre_ref.shape, F32)
        state_ref[...] = jnp.zeros(state_ref.shape, F32)
        r = lax.broadcasted_iota(jnp.int32, (chunk, chunk), 0)
        c = lax.broadcasted_iota(jnp.int32, (chunk, chunk), 1)
        for k in range(n_levels):
            hit = ((r >> (k + 1)) == (c >> (k + 1))) & (((r >> k) & 1) == 1) & (((c >> k) & 1) == 0)
            mask_ref[k] = jnp.where(hit, 1.0, 0.0).astype(BF16)

    def conv_silu(idx, raw_ref, cw_ref, sl, anchor):
        xbuf_ref[idx, pad:pad + chunk, sl] = raw_ref[0, :, sl].astype(F32)
        cw = cw_ref[:, sl] + anchor
        acc = xbuf_ref[idx, pad:pad + chunk, sl] * cw[CONV_WIDTH - 1:CONV_WIDTH]
        for back in range(1, CONV_WIDTH):
            tap = CONV_WIDTH - 1 - back
            acc = acc + xbuf_ref[idx, pad - back:pad - back + chunk, sl] * cw[tap:tap + 1]
        xbuf_ref[idx, 0:pad, sl] = xbuf_ref[idx, chunk:chunk + pad, sl]
        return _silu(acc)

    small = small_ref[0]
    beta_all = _sigmoid(small)
    sp_in = small + dtb_ref[...]
    softplus = jnp.maximum(sp_in, 0.0) + jnp.log(1.0 + jnp.exp(-jnp.abs(sp_in)))
    g_all = -jnp.exp(alog_ref[...]) * softplus
    row = lax.broadcasted_iota(jnp.int32, (chunk, chunk), 0)
    col = lax.broadcasted_iota(jnp.int32, (chunk, chunk), 1)
    incl = col <= row
    tri = jnp.where(incl, 1.0, 0.0).astype(F32)
    gc_all = jnp.dot(tri, g_all, preferred_element_type=F32, precision=HIGHEST)
    eye = jnp.where(row == col, 1.0, 0.0).astype(F32)

    nw = nw_ref[...]
    hs = range(heads)
    sls = [slice(hh * LANES, (hh + 1) * LANES) for hh in hs]

    def prepare(hh, anchor):
        sl = sls[hh]
        qh = conv_silu(0, q_ref, cwq_ref, sl, anchor)
        kh = conv_silu(1, k_ref, cwk_ref, sl, anchor)
        pre_ref[0, :, sl] = qh * lax.rsqrt(jnp.sum(qh * qh, axis=-1, keepdims=True) + NORM_EPS) * (GDN_DK ** -0.5)
        pre_ref[1, :, sl] = kh * lax.rsqrt(jnp.sum(kh * kh, axis=-1, keepdims=True) + NORM_EPS)
        pre_ref[2, :, sl] = conv_silu(2, v_ref, cwv_ref, sl, anchor)

    todo = list(hs)

    def prepare_some(count, after):
        anchor = after[0:CONV_WIDTH, 0:LANES] * 0.0
        for _ in range(min(count, len(todo))):
            prepare(todo.pop(0), anchor)

    qn = [pre_ref[0, :, sls[hh]] for hh in hs]
    kn = [pre_ref[1, :, sls[hh]] for hh in hs]
    vc = [pre_ref[2, :, sls[hh]] for hh in hs]
    kb, gcs, egs, decay = [], [], [], []
    for hh in hs:
        kb.append(kn[hh] * beta_all[:, hh:hh + 1])
        gc = gc_all[:, heads + hh:heads + hh + 1]
        gcs.append(gc)
        egs.append(jnp.exp(gc))
        g_col = jnp.broadcast_to(gc, (chunk, chunk))
        decay.append(jnp.where(incl, jnp.exp(jnp.where(incl, g_col - g_col.T, 0.0)), 0.0))
    kn16 = [kn[hh].astype(BF16) for hh in hs]
    kk = [_dot_nt(kb[hh].astype(BF16), kn16[hh]) for hh in hs]
    qk = [_dot_nt(qn[hh].astype(BF16), kn16[hh]) for hh in hs]
    low = [(kk[hh] * decay[hh]).astype(BF16) for hh in hs]
    intra = [(qk[hh] * decay[hh]).astype(BF16) for hh in hs]
    inv = [eye - (low[hh] * mask_ref[0]).astype(F32) for hh in hs]
    for k in range(1, n_levels):
        inv16 = [inv[hh].astype(BF16) for hh in hs]
        t = [_dot(low[hh] * mask_ref[k], inv16[hh]).astype(BF16) for hh in hs]
        inv = [inv[hh] - _dot(inv16[hh], t[hh]) for hh in hs]
        prepare_some(1, inv[0])
    rhs = [jnp.concatenate([vc[hh] * beta_all[:, hh:hh + 1], kb[hh] * egs[hh]], axis=1).astype(BF16)
           for hh in hs]
    uw = [_dot(inv[hh].astype(BF16), rhs[hh]) for hh in hs]
    prepare_some(heads, uw[0])
    state = [state_ref[hh] for hh in hs]
    state16 = [state[hh].astype(BF16) for hh in hs]
    v_new = [(uw[hh][:, 0:LANES] - _dot(uw[hh][:, LANES:2 * LANES].astype(BF16), state16[hh])).astype(BF16)
             for hh in hs]
    for hh in hs:
        g_last = gcs[hh][chunk - 1:chunk, :]
        kdec = kn[hh] * jnp.exp(g_last - gcs[hh])
        state_ref[hh] = state[hh] * jnp.exp(g_last) + _dot(kdec.T.astype(BF16), v_new[hh])
    for hh in hs:
        o = _dot((qn[hh] * egs[hh]).astype(BF16), state16[hh]) + _dot(intra[hh], v_new[hh])
        y = o * lax.rsqrt(jnp.mean(o * o, axis=-1, keepdims=True) + NORM_EPS) * nw
        zf = z_ref[0, :, sls[hh]].astype(F32)
        o_ref[0, :, sls[hh]] = (y * _silu(zf)).astype(BF16)


def _gated_deltanet(proj3, small3, conv_w, a_log, dt_bias, norm_w):
    bsz, seq, _ = proj3.shape
    heads = GDN_HEADS
    width = heads * LANES
    chunk = min(GDN_CHUNK, seq)
    zeros = jnp.zeros((LANES - 2 * heads,), F32)
    alog_row = jnp.concatenate([jnp.zeros((heads,), F32), a_log, zeros]).reshape(1, LANES)
    dtb_row = jnp.concatenate([jnp.zeros((heads,), F32), dt_bias, zeros]).reshape(1, LANES)
    kern = functools.partial(_gdn_kernel, heads=heads, chunk=chunk)
    n_chunks = seq // chunk
    nxt = lambda s: jnp.minimum(s, n_chunks - 1)
    cur = lambda s: jnp.maximum(s - 1, 0)
    raw = lambda off: pl.BlockSpec((1, chunk, width), lambda b, s: (b, nxt(s), off // width))
    cw = lambda j: pl.BlockSpec((CONV_WIDTH, width), lambda b, s: (0, j))
    row = pl.BlockSpec((1, LANES), lambda b, s: (0, 0))
    return pl.pallas_call(
        kern,
        out_shape=jax.ShapeDtypeStruct((bsz, seq, width), BF16),
        grid=(bsz, n_chunks + 1),
        in_specs=[raw(C_QB), raw(C_KB), raw(C_VB),
                  pl.BlockSpec((1, chunk, width), lambda b, s: (b, cur(s), C_ZB // width)),
                  pl.BlockSpec((1, chunk, LANES), lambda b, s: (b, cur(s), 0)),
                  cw(0), cw(1), cw(2), row, row, row],
        out_specs=pl.BlockSpec((1, chunk, width), lambda b, s: (b, cur(s), 0)),
        scratch_shapes=[pltpu.VMEM((3, chunk + 8, width), F32),
                        pltpu.VMEM((3, chunk, width), F32),
                        pltpu.VMEM((heads, GDN_DK, LANES), F32),
                        pltpu.VMEM((chunk.bit_length() - 1, chunk, chunk), BF16)],
        compiler_params=_cparams(("parallel", "arbitrary")),
        name="gated_deltanet",
    )(proj3, proj3, proj3, proj3, small3, conv_w, conv_w, conv_w, alog_row, dtb_row, norm_w)


def _pack_bf16_pairs(x):
    k = x.shape[1] // 2
    r = pltpu.bitcast(x.astype(BF16).astype(F32), jnp.uint32)
    return (r[:, k:] & jnp.uint32(0xFFFF0000)) | (r[:, :k] >> 16)


def _unpack_bf16_pairs(p):
    lo = pltpu.bitcast(p << 16, F32)
    hi = pltpu.bitcast(p & jnp.uint32(0xFFFF0000), F32)
    return jnp.concatenate([lo, hi], axis=1).astype(BF16)


def _merge_kernel(ya_ref, yb_ref, ga_ref, gb_ref, x_ref, gate1_ref, nw_ref, shift_ref, scale_ref,
                  wa_ref, wb_ref, wo_ref, wr_ref, xo_ref, h2_ref, lg_ref):
    tm = x_ref.shape[0]
    sub = min(MERGE_SUB, tm)
    rows = [slice(i * sub, (i + 1) * sub) for i in range(tm // sub)]
    a = [_dot(ya_ref[r, :], wa_ref[...]) for r in rows]
    b = [_dot(yb_ref[r, :], wb_ref[...]) for r in rows]
    mixed = [(_sigmoid(ga_ref[r, :].astype(F32)) * a[i] + _sigmoid(gb_ref[r, :].astype(F32)) * b[i]).astype(BF16)
             for i, r in enumerate(rows)]
    upd = [_dot(m, wo_ref[...]) for m in mixed]
    for i, r in enumerate(rows):
        xn = x_ref[r, :] + gate1_ref[0] * upd[i]
        xo_ref[r, :] = xn
        ms = jnp.mean(xn * xn, axis=-1, keepdims=True)
        h2 = xn * lax.rsqrt(ms + NORM_EPS) * nw_ref[...]
        h2 = h2 * (1.0 + scale_ref[0]) + shift_ref[0]
        h2_ref[r, :] = _pack_bf16_pairs(h2)
        lg_ref[r, :] = _dot_split(h2, wr_ref)


def _merge_out(ya, yb, proj, x2, gate1, norm_w, shift, scale, wa, wb, wo, wr, seq):
    n, d = x2.shape
    tm = min(1024, seq)
    per_b = seq // tm
    g0 = C_GATES // d
    tok = lambda j: pl.BlockSpec((tm, d), lambda i: (i, j))
    per_batch = pl.BlockSpec((1, 1, d), lambda i: (i // per_b, 0, 0))
    full = lambda r, c: pl.BlockSpec((r, c), lambda i: (0, 0))
    return pl.pallas_call(
        _merge_kernel,
        out_shape=(jax.ShapeDtypeStruct((n, d), F32),
                   jax.ShapeDtypeStruct((n, d // 2), jnp.uint32),
                   jax.ShapeDtypeStruct((n, LANES), F32)),
        grid=(n // tm,),
        in_specs=[tok(0), tok(0), tok(g0), tok(g0 + 1), tok(0), per_batch, full(1, d), per_batch, per_batch,
                  full(d, d), full(d, d), full(d, d), full(d, 2 * LANES)],
        out_specs=(tok(0), pl.BlockSpec((tm, d // 2), lambda i: (i, 0)),
                   pl.BlockSpec((tm, LANES), lambda i: (i, 0))),
        compiler_params=_cparams(("parallel",)),
        name="merge_out",
    )(ya, yb, proj, proj, x2, gate1, norm_w, shift, scale, wa, wb, wo, wr)


def _dispatch_kernel(d0_ref, d1_ref, h2_ref, xs_in_ref, xs_ref, sem, *, tile):
    del xs_in_ref
    d_refs = (d0_ref, d1_ref)

    def row_copy(t, k):
        return pltpu.make_async_copy(h2_ref.at[pl.ds(t, 1), :],
                                     xs_ref.at[pl.ds(d_refs[k][t], 1), :], sem)

    def start(t, carry):
        for k in range(TOP_K):
            row_copy(t, k).start()
        return carry

    def wait(t, carry):
        for k in range(TOP_K):
            row_copy(t, k).wait()
        return carry

    lax.fori_loop(0, tile, start, 0, unroll=DMA_UNROLL)
    lax.fori_loop(0, tile, wait, 0, unroll=DMA_UNROLL)


def _dispatch(dests, h2p, n_rows):
    n, half = h2p.shape
    tile = min(DISPATCH_TILE, n)
    xs0 = jnp.zeros((n_rows, half), jnp.uint32)
    kern = functools.partial(_dispatch_kernel, tile=tile)
    idx = pl.BlockSpec((tile,), lambda i: (i,), memory_space=pltpu.SMEM)
    return pl.pallas_call(
        kern,
        out_shape=jax.ShapeDtypeStruct((n_rows, half), jnp.uint32),
        grid=(n // tile,),
        in_specs=[idx, idx,
                  pl.BlockSpec((tile, half), lambda i: (i, 0)),
                  pl.BlockSpec(memory_space=pl.ANY)],
        out_specs=pl.BlockSpec(memory_space=pl.ANY),
        scratch_shapes=[pltpu.SemaphoreType.DMA],
        input_output_aliases={3: 0},
        compiler_params=_cparams(("arbitrary",)),
        name="moe_dispatch",
    )(dests[0], dests[1], h2p, xs0)


def _expert_kernel(be_ref, used_ref, x_ref, w1_ref, w3_ref, w2_ref, y_ref, w1b_ref, w3b_ref, w2b_ref):
    i = pl.program_id(0)
    live = i < used_ref[0]
    new_expert = jnp.logical_or(i == 0, be_ref[i] != be_ref[jnp.maximum(i - 1, 0)])

    @pl.when(new_expert)
    def _():
        w1b_ref[...] = w1_ref[0, 0].astype(BF16)
        w3b_ref[...] = w3_ref[0, 0].astype(BF16)
        w2b_ref[...] = w2_ref[0, 0].astype(BF16)

    @pl.when(live)
    def _():
        x = _unpack_bf16_pairs(x_ref[...])
        a = _dot(x, w1b_ref[...])
        g = _dot(x, w3b_ref[...])
        y_ref[...] = _dot((_silu(a) * g).astype(BF16), w2b_ref[...])

    @pl.when(jnp.logical_not(live))
    def _():
        y_ref[...] = jnp.zeros(y_ref.shape, F32)


def _experts(block_e, used, xs, w1, w3, w2, layer):
    n_rows, half = xs.shape
    d = 2 * half
    n_blocks = n_rows // MOE_ROWS
    live = lambda i, be, used: jnp.minimum(i, used[0] - 1)
    wspec = lambda r, c: pl.BlockSpec((1, 1, r, c), lambda i, be, used: (layer, be[i], 0, 0))
    return pl.pallas_call(
        _expert_kernel,
        out_shape=jax.ShapeDtypeStruct((n_rows, d), F32),
        grid_spec=pltpu.PrefetchScalarGridSpec(
            num_scalar_prefetch=2,
            grid=(n_blocks,),
            in_specs=[pl.BlockSpec((MOE_ROWS, half), lambda i, be, used: (live(i, be, used), 0)),
                      wspec(d, D_EXPERT), wspec(d, D_EXPERT), wspec(D_EXPERT, d)],
            out_specs=pl.BlockSpec((MOE_ROWS, d), lambda i, be, used: (i, 0)),
            scratch_shapes=[pltpu.VMEM((d, D_EXPERT), BF16), pltpu.VMEM((d, D_EXPERT), BF16),
                            pltpu.VMEM((D_EXPERT, d), BF16)]),
        compiler_params=_cparams(("arbitrary",)),
        name="moe_experts",
    )(block_e, used, xs, w1, w3, w2)


def _combine_kernel(c0_ref, c1_ref, n0_ref, n1_ref, ys_ref, x_ref, wt_ref, gate2_ref, fw_ref, o_ref, buf_ref,
                    sems, *, tile, final):
    i = pl.program_id(0)
    slot = i & 1
    cur = (c0_ref, c1_ref)
    nxt = (n0_ref, n1_ref)

    def row_copy(d_refs, t, k, s):
        return pltpu.make_async_copy(ys_ref.at[pl.ds(d_refs[k][t], 1), :],
                                     buf_ref.at[s, k, pl.ds(t, 1), :], sems.at[s])

    def gather(d_refs, s):
        def start(t, carry):
            for k in range(TOP_K):
                row_copy(d_refs, t, k, s).start()
            return carry

        lax.fori_loop(0, tile, start, 0, unroll=DMA_UNROLL)

    @pl.when(i == 0)
    def _():
        gather(cur, 0)

    @pl.when(i < pl.num_programs(0) - 1)
    def _():
        gather(nxt, 1 - slot)

    def wait(t, carry):
        for k in range(TOP_K):
            row_copy(cur, t, k, slot).wait()
        return carry

    lax.fori_loop(0, tile, wait, 0, unroll=DMA_UNROLL)
    wt = wt_ref[...]
    y = wt[:, 0:1] * buf_ref[slot, 0] + wt[:, 1:2] * buf_ref[slot, 1]
    xn = x_ref[...] + gate2_ref[0] * y
    if final:
        xn = xn * lax.rsqrt(jnp.mean(xn * xn, axis=-1, keepdims=True) + NORM_EPS) * fw_ref[...]
    o_ref[...] = xn


def _combine(dests, ys, x2, wts, gate2, final_w, seq, final):
    n, d = x2.shape
    tile = min(TOK_TILE, seq)
    per_b = seq // tile
    steps = n // tile
    kern = functools.partial(_combine_kernel, tile=tile, final=final)
    cur = pl.BlockSpec((tile,), lambda i: (i,), memory_space=pltpu.SMEM)
    nxt = pl.BlockSpec((tile,), lambda i: (jnp.minimum(i + 1, steps - 1),), memory_space=pltpu.SMEM)
    return pl.pallas_call(
        kern,
        out_shape=jax.ShapeDtypeStruct((n, d), F32),
        grid=(steps,),
        in_specs=[cur, cur, nxt, nxt,
                  pl.BlockSpec(memory_space=pl.ANY),
                  pl.BlockSpec((tile, d), lambda i: (i, 0)),
                  pl.BlockSpec((tile, LANES), lambda i: (i, 0)),
                  pl.BlockSpec((1, 1, d), lambda i: (i // per_b, 0, 0)),
                  pl.BlockSpec((1, d), lambda i: (0, 0))],
        out_specs=pl.BlockSpec((tile, d), lambda i: (i, 0)),
        scratch_shapes=[pltpu.VMEM((2, TOP_K, tile, d), F32), pltpu.SemaphoreType.DMA((2,))],
        compiler_params=_cparams(("arbitrary",)),
        name="moe_combine",
    )(dests[0], dests[1], dests[0], dests[1], ys, x2, wts, gate2, final_w)


def _route_kernel(lg_ref, bias_ref, dest_ref, wt_ref, cnt_ref, tri_ref, base_ref, start_ref, *, tl):
    phase = pl.program_id(0)
    i = pl.program_id(1)
    ne = N_EXPERTS
    eg = EXPERTS_PER_GROUP

    @pl.when(jnp.logical_and(phase == 0, i == 0))
    def _():
        base_ref[...] = jnp.zeros(base_ref.shape, F32)
        r = lax.broadcasted_iota(jnp.int32, (tl, tl), 0)
        c = lax.broadcasted_iota(jnp.int32, (tl, tl), 1)
        tri_ref[...] = jnp.where(r < c, 1.0, 0.0).astype(BF16)

    lt = lg_ref[...].T + bias_ref[...]
    gl = lt[0:N_GROUPS]
    gmax = jnp.max(gl, axis=0, keepdims=True)
    gsum = jnp.sum(jnp.exp(gl - gmax), axis=0, keepdims=True)
    grow = lax.broadcasted_iota(jnp.int32, (N_GROUPS, tl), 0)
    gsel = jnp.min(jnp.where(gl == gmax, grow, N_GROUPS), axis=0, keepdims=True)
    gweight = 1.0 / gsum
    el = lt[8:8 + ne]
    sel = el[0:eg]
    for g in range(1, N_GROUPS):
        sel = jnp.where(gsel == g, el[g * eg:(g + 1) * eg], sel)
    erow = lax.broadcasted_iota(jnp.int32, (eg, tl), 0)
    v1 = jnp.max(sel, axis=0, keepdims=True)
    i1 = jnp.min(jnp.where(sel == v1, erow, eg), axis=0, keepdims=True)
    rest = jnp.where(erow == i1, NEG_INF, sel)
    v2 = jnp.max(rest, axis=0, keepdims=True)
    i2 = jnp.min(jnp.where(rest == v2, erow, eg), axis=0, keepdims=True)
    esum = jnp.sum(jnp.exp(sel - v1), axis=0, keepdims=True)
    p1 = 1.0 / esum
    p2 = jnp.exp(v2 - v1) / esum
    w1 = gweight * p1 / (p1 + p2)
    w2 = gweight * p2 / (p1 + p2)
    eio = lax.broadcasted_iota(jnp.int32, (ne, tl), 0)
    oh1 = jnp.where(eio == gsel * eg + i1, 1.0, 0.0)
    oh2 = jnp.where(eio == gsel * eg + i2, 1.0, 0.0)
    oh = oh1 + oh2
    tile_cnt = jnp.sum(oh, axis=1, keepdims=True)

    @pl.when(phase == 0)
    def _():
        base_ref[...] = base_ref[...] + tile_cnt

    @pl.when(jnp.logical_and(phase == 1, i == 0))
    def _():
        cnt = base_ref[...]
        cnt_ref[...] = cnt
        padded = jnp.floor((cnt + (MOE_ROWS - 1)) * (1.0 / MOE_ROWS)) * MOE_ROWS
        r = lax.broadcasted_iota(jnp.int32, (ne, ne), 0)
        c = lax.broadcasted_iota(jnp.int32, (ne, ne), 1)
        below = jnp.where(c < r, 1.0, 0.0).astype(F32)
        start_ref[...] = jnp.dot(below, padded, preferred_element_type=F32, precision=HIGHEST)
        base_ref[...] = jnp.zeros(base_ref.shape, F32)

    @pl.when(phase == 1)
    def _():
        earlier = _dot(oh.astype(BF16), tri_ref[...]) + base_ref[:, 0:1]
        pos = earlier + start_ref[:, 0:1]
        d1 = jnp.sum(oh1 * pos, axis=0, keepdims=True)
        d2 = jnp.sum(oh2 * pos, axis=0, keepdims=True)
        row8 = lax.broadcasted_iota(jnp.int32, (8, tl), 0)
        dest_ref[...] = jnp.where(row8 == 0, d1, jnp.where(row8 == 1, d2, 0.0)).astype(jnp.int32)
        rowl = lax.broadcasted_iota(jnp.int32, (LANES, tl), 0)
        wt_ref[...] = jnp.where(rowl == 0, w1, jnp.where(rowl == 1, w2, 0.0)).T
        base_ref[...] = base_ref[...] + tile_cnt


def _route(logits, b_rg, b_re):
    n = logits.shape[0]
    tl = min(ROUTE_TILE, n)
    bias = jnp.zeros((LANES,), F32).at[0:N_GROUPS].set(b_rg).at[8:8 + N_EXPERTS].set(b_re).reshape(LANES, 1)
    kern = functools.partial(_route_kernel, tl=tl)
    dest, wts, cnt = pl.pallas_call(
        kern,
        out_shape=(jax.ShapeDtypeStruct((8, n), jnp.int32), jax.ShapeDtypeStruct((n, LANES), F32),
                   jax.ShapeDtypeStruct((N_EXPERTS, LANES), F32)),
        grid=(2, n // tl),
        in_specs=[pl.BlockSpec((tl, LANES), lambda p, i: (i, 0)),
                  pl.BlockSpec((LANES, 1), lambda p, i: (0, 0))],
        out_specs=(pl.BlockSpec((8, tl), lambda p, i: (0, i * p)),
                   pl.BlockSpec((tl, LANES), lambda p, i: (i * p, 0)),
                   pl.BlockSpec((N_EXPERTS, LANES), lambda p, i: (0, 0))),
        scratch_shapes=[pltpu.VMEM((tl, tl), BF16), pltpu.VMEM((N_EXPERTS, LANES), F32),
                        pltpu.VMEM((N_EXPERTS, LANES), F32)],
        compiler_params=_cparams(("arbitrary", "arbitrary")),
        name="moe_route",
    )(logits, bias)
    counts = cnt[:, 0].astype(jnp.int32)
    padded = (counts + MOE_ROWS - 1) // MOE_ROWS * MOE_ROWS
    pad_end = jnp.cumsum(padded)
    n_blocks = (n * TOP_K) // MOE_ROWS + N_EXPERTS
    used = jnp.maximum(pad_end[-1] // MOE_ROWS, 1).astype(jnp.int32)
    blk = jnp.minimum(jnp.arange(n_blocks, dtype=jnp.int32), used - 1)
    block_e = jnp.sum((pad_end[None, :] <= (blk * MOE_ROWS)[:, None]).astype(jnp.int32), axis=1)
    block_e = jnp.minimum(block_e, N_EXPERTS - 1).astype(jnp.int32)
    return (dest[0], dest[1]), wts, block_e, used.reshape(1), n_blocks * MOE_ROWS


def kernel(x, c, ada_w, ada_b, norm1_w, w_in, conv_w, lambda_q1, lambda_k1, lambda_q2, lambda_k2,
           subln_w, a_log, dt_bias, gdn_norm_w, w_branch_a, w_branch_b, w_out, norm2_w,
           router_group_w, router_group_b, router_expert_w, router_expert_b,
           expert_w1, expert_w3, expert_w2, final_norm_w):
    bsz, seq, d = x.shape
    depth = ada_w.shape[0]
    n = bsz * seq
    mod = _ada_mod(c, ada_w, ada_b)
    x2 = x.reshape(n, d)
    n_beta = 7168
    for l in range(depth):
        lam_init = 0.8 - 0.6 * math.exp(-0.3 * l)
        m = mod[l].reshape(bsz, 1, 6, d)
        shift1, scale1, gate1, shift2, scale2, gate2 = (m[:, :, i] for i in range(6))
        w = w_in[l]
        w_main = jnp.concatenate([w[:, :n_beta], w[:, n_beta + 2 * GDN_HEADS:]], axis=1).astype(BF16)
        w_small = _split_hi_lo(jnp.pad(w[:, n_beta:n_beta + 2 * GDN_HEADS], ((0, 0), (0, LANES - 2 * GDN_HEADS))))
        proj, small = _in_proj(x2, shift1, scale1, norm1_w[l].reshape(1, d), w_main, w_small, seq)
        proj3 = proj.reshape(bsz, seq, P_MAIN)
        lamv = jnp.pad(jnp.stack([lambda_q1[l], lambda_k1[l], lambda_q2[l], lambda_k2[l]]),
                       ((0, 0), (0, LANES - ATT_HEAD_DIM)))
        ya = _diff_attention(proj3, lamv, subln_w[l].reshape(1, LANES), lam_init)
        yb = _gated_deltanet(proj3, small.reshape(bsz, seq, LANES), conv_w[l], a_log[l], dt_bias[l],
                             gdn_norm_w[l].reshape(1, LANES))
        wr = jnp.zeros((d, LANES), F32)
        wr = wr.at[:, 0:N_GROUPS].set(router_group_w[l]).at[:, 8:8 + N_EXPERTS].set(router_expert_w[l])
        x2, h2p, logits = _merge_out(
            ya.reshape(n, d), yb.reshape(n, d), proj, x2, gate1, norm2_w[l].reshape(1, d), shift2, scale2,
            w_branch_a[l].astype(BF16), w_branch_b[l].astype(BF16), w_out[l].astype(BF16), _split_hi_lo(wr), seq)
        dest, wts, block_e, used, n_rows = _route(logits, router_group_b[l], router_expert_b[l])
        xs = _dispatch(dest, h2p, n_rows)
        ys = _experts(block_e, used, xs, expert_w1, expert_w3, expert_w2, l)
        x2 = _combine(dest, ys, x2, wts, gate2, final_norm_w.reshape(1, d), seq, final=(l == depth - 1))
    return x2.reshape(bsz, seq, d)
```
